```python
import jax, jax.numpy as jnp
from jax import lax
import numpy as np

D_MODEL = 4096
BATCH = 8
SEQ = 2048
DEPTH = 2

N_MIXERS = 2
POOL_WINDOWS = (2, 4, 8, 16)
N_POOL_GROUPS = len(POOL_WINDOWS)
POOL_GROUP = D_MODEL // N_POOL_GROUPS
GLA_HEADS = 4
GLA_KEY_DIM = D_MODEL // 2
GLA_VAL_DIM = D_MODEL
GLA_HEAD_K = GLA_KEY_DIM // GLA_HEADS
GLA_HEAD_V = GLA_VAL_DIM // GLA_HEADS
GATE_RANK = 16
GATE_TAU = 16.0
CHUNK = 64
D_FF = 4 * D_MODEL
EPS = 1e-6
N_POOL_LAYERS = (DEPTH + 1) // 2
N_GLA_LAYERS = DEPTH // 2
PROJ_WIDTH = 2 * GLA_KEY_DIM + 2 * GLA_VAL_DIM + 2 * GATE_RANK

kernel_name = "hybrid_pool_gla_encoder"


def rmsnorm(x, g):
    xf = x.astype(jnp.float32)
    y = xf * lax.rsqrt(jnp.mean(xf * xf, axis=-1, keepdims=True) + EPS)
    return (y * g.astype(jnp.float32)).astype(x.dtype)


def pool_mixer(h, w_group, scale):
    B, S, D = h.shape
    hf = h.astype(jnp.float32)
    csum = jnp.concatenate([jnp.zeros((B, 1, D), jnp.float32), jnp.cumsum(hf, axis=1)], axis=1)
    pos = jnp.arange(S)
    diffs = []
    for g, w in enumerate(POOL_WINDOWS):
        lo = jnp.clip(pos - w // 2, 0, S)
        hi = jnp.clip(pos + w // 2, 0, S)
        sl = slice(g * POOL_GROUP, (g + 1) * POOL_GROUP)
        cs = csum[..., sl]
        count = (hi - lo).astype(jnp.float32)[None, :, None]
        mean = (cs[:, hi] - cs[:, lo]) / count
        diffs.append(mean - hf[..., sl])
    d = jnp.stack(diffs, axis=2).astype(h.dtype)
    y = jnp.einsum('bsgc,gcd->bsgd', d, w_group).reshape(B, S, D)
    return y * scale


def gla_chunked(q, k, v, log_a):
    B, S, H, dk = q.shape
    dv = v.shape[-1]
    n_chunks = S // CHUNK

    def to_chunks(t):
        return t.astype(jnp.float32).reshape(B, n_chunks, CHUNK, H, -1).transpose(1, 0, 3, 2, 4)

    qc, kc, vc, gc = to_chunks(q), to_chunks(k), to_chunks(v), to_chunks(log_a)
    b = jnp.cumsum(gc, axis=3)
    b_last = b[..., CHUNK - 1:, :]
    b_mid = b[..., CHUNK // 2 - 1:CHUNK // 2, :]
    scores = jnp.einsum('nbhik,nbhjk->nbhij', qc * jnp.exp(b - b_mid), kc * jnp.exp(b_mid - b))
    mask = jnp.tril(jnp.ones((CHUNK, CHUNK), dtype=bool))
    scores = jnp.where(mask, scores, 0.0)
    o_intra = jnp.einsum('nbhij,nbhjv->nbhiv', scores, vc)
    q_inter = qc * jnp.exp(b)
    k_state = kc * jnp.exp(b_last - b)
    a_chunk = jnp.exp(b_last[..., 0, :])

    def step(state, inp):
        qi, ki, vi, ai = inp
        o = jnp.einsum('bhik,bhkv->bhiv', qi, state)
        state = ai[..., None] * state + jnp.einsum('bhjk,bhjv->bhkv', ki, vi)
        return state, o

    state0 = jnp.zeros((B, H, dk, dv), jnp.float32)
    _, o_inter = lax.scan(step, state0, (q_inter, k_state, vc, a_chunk))
    o = o_intra + o_inter
    return o.transpose(1, 0, 3, 2, 4).reshape(B, S, H, dv)


def gla_mixer(h, w_in, w_up_f, b_up_f, w_up_b, b_up_b, g_norm, w_out):
    B, S, D = h.shape
    p = h @ w_in
    c1 = GLA_KEY_DIM
    c2 = 2 * GLA_KEY_DIM
    c3 = c2 + GLA_VAL_DIM
    c4 = c3 + GLA_VAL_DIM
    c5 = c4 + GATE_RANK
    q, k, v, gate, r_f, r_b = jnp.split(p, [c1, c2, c3, c4, c5], axis=-1)
    q = q.reshape(B, S, GLA_HEADS, GLA_HEAD_K) * (GLA_HEAD_K ** -0.5)
    k = k.reshape(B, S, GLA_HEADS, GLA_HEAD_K)
    v = v.reshape(B, S, GLA_HEADS, GLA_HEAD_V)
    log_a_f = jax.nn.log_sigmoid((r_f @ w_up_f + b_up_f).astype(jnp.float32)) / GATE_TAU
    log_a_b = jax.nn.log_sigmoid((r_b @ w_up_b + b_up_b).astype(jnp.float32)) / GATE_TAU
    log_a_f = log_a_f.reshape(B, S, GLA_HEADS, GLA_HEAD_K)
    log_a_b = log_a_b.reshape(B, S, GLA_HEADS, GLA_HEAD_K)
    o_fwd = gla_chunked(q, k, v, log_a_f)
    flip = lambda t: jnp.flip(t, axis=1)
    o_bwd = flip(gla_chunked(flip(q), flip(k), flip(v), flip(log_a_b)))
    o = o_fwd + o_bwd
    o = o * lax.rsqrt(jnp.mean(o * o, axis=-1, keepdims=True) + EPS) * g_norm.astype(jnp.float32)
    o = o.reshape(B, S, GLA_VAL_DIM) * jax.nn.silu(gate.astype(jnp.float32))
    return o.astype(h.dtype) @ w_out


def relu2_mlp(h, w1, w2):
    a = jax.nn.relu(h @ w1)
    return (a * a) @ w2


def setup_inputs(seed: int = 0) -> dict:
    key = jax.random.key(seed)
    ks = jax.random.split(key, 16)
    f32 = jnp.float32
    nrm = lambda k, shape, s: (jax.random.normal(k, shape, f32) * s).astype(f32)
    return {
        "x": nrm(ks[0], (BATCH, SEQ, D_MODEL), 1.0),
        "norm_mix": 1.0 + nrm(ks[1], (DEPTH, D_MODEL), 0.02),
        "norm_mlp": 1.0 + nrm(ks[2], (DEPTH, D_MODEL), 0.02),
        "norm_final": 1.0 + nrm(ks[3], (D_MODEL,), 0.02),
        "pool_w": nrm(ks[4], (N_POOL_LAYERS, N_POOL_GROUPS, POOL_GROUP, POOL_GROUP), POOL_GROUP ** -0.5),
        "pool_scale": 1.0 + nrm(ks[5], (N_POOL_LAYERS, D_MODEL), 0.02),
        "gla_w_in": nrm(ks[6], (N_GLA_LAYERS, D_MODEL, PROJ_WIDTH), D_MODEL ** -0.5),
        "gla_w_up_f": nrm(ks[7], (N_GLA_LAYERS, GATE_RANK, GLA_KEY_DIM), GATE_RANK ** -0.5),
        "gla_b_up_f": nrm(ks[8], (N_GLA_LAYERS, GLA_KEY_DIM), 0.1),
        "gla_w_up_b": nrm(ks[9], (N_GLA_LAYERS, GATE_RANK, GLA_KEY_DIM), GATE_RANK ** -0.5),
        "gla_b_up_b": nrm(ks[10], (N_GLA_LAYERS, GLA_KEY_DIM), 0.1),
        "gla_g_norm": 1.0 + nrm(ks[11], (N_GLA_LAYERS, GLA_HEAD_V), 0.02),
        "gla_w_out": nrm(ks[12], (N_GLA_LAYERS, GLA_VAL_DIM, D_MODEL), GLA_VAL_DIM ** -0.5),
        "mlp_w_in": nrm(ks[13], (DEPTH, D_MODEL, D_FF), D_MODEL ** -0.5),
        "mlp_w_out": nrm(ks[14], (DEPTH, D_FF, D_MODEL), D_FF ** -0.5),
    }


def reference(x, norm_mix, norm_mlp, norm_final, pool_w, pool_scale, gla_w_in, gla_w_up_f,
              gla_b_up_f, gla_w_up_b, gla_b_up_b, gla_g_norm, gla_w_out, mlp_w_in, mlp_w_out):
    h = x
    for layer in range(DEPTH):
        j = layer // N_MIXERS
        hn = rmsnorm(h, norm_mix[layer])
        if layer % N_MIXERS == 0:
            h = h + pool_mixer(hn, pool_w[j], pool_scale[j])
        else:
            h = h + gla_mixer(hn, gla_w_in[j], gla_w_up_f[j], gla_b_up_f[j], gla_w_up_b[j],
                              gla_b_up_b[j], gla_g_norm[j], gla_w_out[j])
        hn = rmsnorm(h, norm_mlp[layer])
        h = h + relu2_mlp(hn, mlp_w_in[layer], mlp_w_out[layer])
    return rmsnorm(h, norm_final)
```

```python
import functools

import jax
import jax.numpy as jnp
from jax import lax
from jax.experimental import pallas as pl
from jax.experimental.pallas import tpu as pltpu

EPS = 1e-6
POOL_WINDOWS = (2, 4, 8, 16)
GLA_HEADS = 4
GATE_RANK = 16
GATE_TAU = 16.0
CHUNK = 64

V7X_VMEM_BYTES = 64 * 1024 * 1024
LANES = 128
SUBLANES = 8

F32 = jnp.float32
BF16 = jnp.bfloat16


def _vmem_limit(block_bytes):
    return int(min(V7X_VMEM_BYTES - 4 * 1024 * 1024, 2 * block_bytes))


def _nbytes(shape, dtype):
    n = 1
    for s in shape:
        n *= s
    return n * jnp.dtype(dtype).itemsize


def _rmsnorm_rows(x, g):
    y = x * lax.rsqrt(jnp.mean(x * x, axis=-1, keepdims=True) + EPS)
    return y * g


def _rmsnorm_kernel(x_ref, g_ref, o_ref):
    o_ref[...] = _rmsnorm_rows(x_ref[...], g_ref[...]).astype(o_ref.dtype)


def rmsnorm(x, g, out_dtype, tm=512):
    m, d = x.shape
    blocks = 2 * (_nbytes((tm, d), x.dtype) + _nbytes((tm, d), out_dtype))
    return pl.pallas_call(
        _rmsnorm_kernel,
        grid=(m // tm,),
        in_specs=[pl.BlockSpec((tm, d), lambda i: (i, 0)), pl.BlockSpec((1, d), lambda i: (0, 0))],
        out_specs=pl.BlockSpec((tm, d), lambda i: (i, 0)),
        out_shape=jax.ShapeDtypeStruct((m, d), out_dtype),
        compiler_params=pltpu.CompilerParams(
            dimension_semantics=("arbitrary",), vmem_limit_bytes=_vmem_limit(blocks)),
        name="rmsnorm",
    )(x, g.reshape(1, d))


def _mm_kernel(x_ref, w_ref, *rest, relu2, residual):
    o_ref = rest[-1]
    acc = jnp.dot(x_ref[...], w_ref[...], preferred_element_type=F32)
    if relu2:
        acc = jnp.maximum(acc, 0.0)
        acc = acc * acc
    if residual:
        acc = rest[0][...] + acc
    o_ref[...] = acc.astype(o_ref.dtype)


def matmul(x, w, out_dtype, *, relu2=False, residual=None, tm=1024, tn=1024, name="matmul"):
    m, k = x.shape
    n = w.shape[1]
    tn = min(tn, n)
    in_specs = [pl.BlockSpec((tm, k), lambda i, j: (i, 0)), pl.BlockSpec((k, tn), lambda i, j: (0, j))]
    args = [x, w]
    blocks = _nbytes((tm, k), x.dtype) + _nbytes((k, tn), w.dtype) + _nbytes((tm, tn), out_dtype)
    if residual is not None:
        in_specs.append(pl.BlockSpec((tm, tn), lambda i, j: (i, j)))
        args.append(residual)
        blocks += _nbytes((tm, tn), residual.dtype)
    return pl.pallas_call(
        functools.partial(_mm_kernel, relu2=relu2, residual=residual is not None),
        grid=(m // tm, n // tn),
        in_specs=in_specs,
        out_specs=pl.BlockSpec((tm, tn), lambda i, j: (i, j)),
        out_shape=jax.ShapeDtypeStruct((m, n), out_dtype),
        compiler_params=pltpu.CompilerParams(
            dimension_semantics=("arbitrary", "arbitrary"),
            vmem_limit_bytes=_vmem_limit(2 * blocks + _nbytes((tm, tn), F32))),
        name=name,
    )(*args)


def _mm_ktiled_kernel(x_ref, w_ref, res_ref, o_ref):
    kk = pl.program_id(2)
    part = jnp.dot(x_ref[...], w_ref[...], preferred_element_type=F32)

    @pl.when(kk == 0)
    def _():
        o_ref[...] = res_ref[...] + part

    @pl.when(kk > 0)
    def _():
        o_ref[...] += part


def matmul_ktiled(x, w, residual, *, tm=1024, tn=1024, tk=2048, name="matmul_ktiled"):
    m, k = x.shape
    n = w.shape[1]
    blocks = (_nbytes((tm, tk), x.dtype) + _nbytes((tk, tn), w.dtype) + 2 * _nbytes((tm, tn), F32))
    return pl.pallas_call(
        _mm_ktiled_kernel,
        grid=(m // tm, n // tn, k // tk),
        in_specs=[
            pl.BlockSpec((tm, tk), lambda i, j, kk: (i, kk)),
            pl.BlockSpec((tk, tn), lambda i, j, kk: (kk, j)),
            pl.BlockSpec((tm, tn), lambda i, j, kk: (i, j)),
        ],
        out_specs=pl.BlockSpec((tm, tn), lambda i, j, kk: (i, j)),
        out_shape=jax.ShapeDtypeStruct((m, n), F32),
        compiler_params=pltpu.CompilerParams(
            dimension_semantics=("arbitrary", "arbitrary", "arbitrary"),
            vmem_limit_bytes=_vmem_limit(2 * blocks + _nbytes((tm, tn), F32))),
        name=name,
    )(x, w, residual)


def _pool_kernel(x_ref, xp_ref, xn_ref, gmix_ref, gmlp_ref, w_ref, scale_ref, h_ref, hn_ref, buf_ref,
                 *, tm, seq, halo):
    i = pl.program_id(1)
    nblk = pl.num_programs(1)
    gmix = gmix_ref[...]
    x = x_ref[0]
    d_model = x.shape[-1]
    group = d_model // len(POOL_WINDOWS)

    buf_ref[halo:halo + tm, :] = _rmsnorm_rows(x, gmix)
    buf_ref[0:halo, :] = jnp.where(i > 0, _rmsnorm_rows(xp_ref[0], gmix), 0.0)
    buf_ref[halo + tm:halo + tm + halo, :] = jnp.where(i < nblk - 1, _rmsnorm_rows(xn_ref[0], gmix), 0.0)

    pos = i * tm + lax.broadcasted_iota(jnp.int32, (tm, 1), 0)
    for gi, win in enumerate(POOL_WINDOWS):
        cols = slice(gi * group, (gi + 1) * group)
        half = win // 2
        acc = buf_ref[halo - half:halo - half + tm, cols]
        for off in range(-half + 1, half):
            acc = acc + buf_ref[halo + off:halo + off + tm, cols]
        count = (jnp.minimum(pos + half, seq) - jnp.maximum(pos - half, 0)).astype(F32)
        diff = acc / count - buf_ref[halo:halo + tm, cols]
        y = jnp.dot(diff.astype(BF16), w_ref[gi], preferred_element_type=F32)
        h_ref[0, :, cols] = x[:, cols] + y * scale_ref[:, cols]

    hn_ref[0] = _rmsnorm_rows(h_ref[0], gmlp_ref[...]).astype(hn_ref.dtype)


def pool_layer(x, g_mix, g_mlp, pool_w, pool_scale, tm=256):
    b, s, d = x.shape
    halo = SUBLANES
    assert max(POOL_WINDOWS) // 2 <= halo and tm % halo == 0 and s % tm == 0
    ng, gd, _ = pool_w.shape
    hb = tm // halo
    blocks = (2 * _nbytes((tm, d), F32) * 2 + 2 * _nbytes((tm, d), BF16) + 2 * _nbytes(pool_w.shape, BF16)
              + _nbytes((tm + 2 * halo, d), F32))
    return pl.pallas_call(
        functools.partial(_pool_kernel, tm=tm, seq=s, halo=halo),
        grid=(b, s // tm),
        in_specs=[
            pl.BlockSpec((1, tm, d), lambda bi, i: (bi, i, 0)),
            pl.BlockSpec((1, halo, d), lambda bi, i: (bi, jnp.maximum(i * hb - 1, 0), 0)),
            pl.BlockSpec((1, halo, d), lambda bi, i: (bi, jnp.minimum((i + 1) * hb, s // halo - 1), 0)),
            pl.BlockSpec((1, d), lambda bi, i: (0, 0)),
            pl.BlockSpec((1, d), lambda bi, i: (0, 0)),
            pl.BlockSpec((ng, gd, gd), lambda bi, i: (0, 0, 0)),
            pl.BlockSpec((1, d), lambda bi, i: (0, 0)),
        ],
        out_specs=[
            pl.BlockSpec((1, tm, d), lambda bi, i: (bi, i, 0)),
            pl.BlockSpec((1, tm, d), lambda bi, i: (bi, i, 0)),
        ],
        out_shape=[jax.ShapeDtypeStruct((b, s, d), F32), jax.ShapeDtypeStruct((b, s, d), BF16)],
        scratch_shapes=[pltpu.VMEM((tm + 2 * halo, d), F32)],
        compiler_params=pltpu.CompilerParams(
            dimension_semantics=("arbitrary", "arbitrary"), vmem_limit_bytes=_vmem_limit(blocks)),
        name="pool_layer",
    )(x, x, x, g_mix.reshape(1, d), g_mlp.reshape(1, d), pool_w.astype(BF16), pool_scale.reshape(1, d))


def _split3(x):
    hi = x.astype(BF16)
    r1 = x - hi.astype(F32)
    mid = r1.astype(BF16)
    lo = (r1 - mid.astype(F32)).astype(BF16)
    return hi, mid, lo


def _gla_chunk(q, k, v, z, bias, state_ref, *, reverse):
    c, dk = q.shape
    g = jax.nn.log_sigmoid(z + bias) * (1.0 / GATE_TAU)
    row = lax.broadcasted_iota(jnp.int32, (c, c), 0)
    col = lax.broadcasted_iota(jnp.int32, (c, c), 1)
    keep = (col >= row) if reverse else (col <= row)
    tri = jnp.where(keep, 1.0, 0.0).astype(BF16)
    b = sum(jnp.dot(tri, part, preferred_element_type=F32) for part in _split3(g))
    last_i, mid_i = (0, c - c // 2) if reverse else (c - 1, c // 2 - 1)
    b_last = b[last_i:last_i + 1, :]
    b_mid = b[mid_i:mid_i + 1, :]

    qs = q * (dk ** -0.5)
    vb = v.astype(BF16)
    q_intra = (qs * jnp.exp(b - b_mid)).astype(BF16)
    k_intra = (k * jnp.exp(b_mid - b)).astype(BF16)
    scores = lax.dot_general(q_intra, k_intra, (((1,), (1,)), ((), ())), preferred_element_type=F32)
    scores = jnp.where(keep, scores, 0.0)
    o = jnp.dot(scores.astype(BF16), vb, preferred_element_type=F32)

    q_inter = (qs * jnp.exp(b)).astype(BF16)
    state = state_ref[...]
    o = o + lax.dot_general(q_inter, state.astype(BF16), (((1,), (1,)), ((), ())), preferred_element_type=F32)

    k_state = (k * jnp.exp(b_last - b)).astype(BF16)
    upd = lax.dot_general(vb, k_state, (((0,), (0,)), ((), ())), preferred_element_type=F32)
    state_ref[...] = jnp.exp(b_last) * state + upd
    return o


def _gla_gate_preact(r_ref, wup_ref):
    return jnp.dot(r_ref[0].astype(BF16), wup_ref[...], preferred_element_type=F32)


def _gla_fwd_kernel(q_ref, k_ref, v_ref, r_ref, wup_ref, bias_ref, o_ref, state_ref, z_ref, *, bs):
    @pl.when(pl.program_id(2) == 0)
    def _():
        state_ref[...] = jnp.zeros_like(state_ref)

    z_ref[...] = _gla_gate_preact(r_ref, wup_ref)
    bias = bias_ref[...]

    def body(ci, carry):
        rows = pl.ds(pl.multiple_of(ci * CHUNK, CHUNK), CHUNK)
        o_ref[0, rows, :] = _gla_chunk(q_ref[0, rows, :], k_ref[0, rows, :], v_ref[0, rows, :], z_ref[rows, :],
                                       bias, state_ref, reverse=False)
        return carry

    lax.fori_loop(0, bs // CHUNK, body, 0)


def _gla_bwd_kernel(q_ref, k_ref, v_ref, r_ref, wup_ref, bias_ref, ofwd_ref, gate_ref, gnorm_ref, o_ref,
                    state_ref, z_ref, *, bs):
    @pl.when(pl.program_id(2) == 0)
    def _():
        state_ref[...] = jnp.zeros_like(state_ref)

    z_ref[...] = _gla_gate_preact(r_ref, wup_ref)
    bias = bias_ref[...]
    gnorm = gnorm_ref[...]
    nchunks = bs // CHUNK

    def body(ci, carry):
        rows = pl.ds(pl.multiple_of((nchunks - 1 - ci) * CHUNK, CHUNK), CHUNK)
        o = ofwd_ref[0, rows, :] + _gla_chunk(q_ref[0, rows, :], k_ref[0, rows, :], v_ref[0, rows, :],
                                              z_ref[rows, :], bias, state_ref, reverse=True)
        o = _rmsnorm_rows(o, gnorm)
        o_ref[0, rows, :] = (o * jax.nn.silu(gate_ref[0, rows, :])).astype(o_ref.dtype)
        return carry

    lax.fori_loop(0, nchunks, body, 0)


def gla_core(p, r, wup_f, bias_f, wup_b, bias_b, g_norm, *, heads, bs=256):
    b, s, width = p.shape
    dv_total = width // 3
    dk_total = dv_total // 2
    dk, dv = dk_total // heads, dv_total // heads
    nb = s // bs
    kq = dk_total // dk
    kv = (2 * dk_total) // dv
    kg = (2 * dk_total + dv_total) // dv
    rw = r.shape[-1]

    def specs(blk):
        return [
            pl.BlockSpec((1, bs, dk), lambda bi, h, i: (bi, blk(i), h)),
            pl.BlockSpec((1, bs, dk), lambda bi, h, i: (bi, blk(i), kq + h)),
            pl.BlockSpec((1, bs, dv), lambda bi, h, i: (bi, blk(i), kv + h)),
            pl.BlockSpec((1, bs, rw), lambda bi, h, i: (bi, blk(i), 0)),
            pl.BlockSpec((rw, dk), lambda bi, h, i: (0, h)),
            pl.BlockSpec((1, dk), lambda bi, h, i: (0, h)),
        ]

    scratch = [pltpu.VMEM((dv, dk), F32), pltpu.VMEM((bs, dk), F32)]
    in_blocks = 2 * _nbytes((bs, dk), F32) + _nbytes((bs, dv), F32) + _nbytes((bs, rw), F32)
    scratch_bytes = _nbytes((dv, dk), F32) + _nbytes((bs, dk), F32)
    params = functools.partial(pltpu.CompilerParams, dimension_semantics=("arbitrary", "arbitrary", "arbitrary"))

    o_fwd = pl.pallas_call(
        functools.partial(_gla_fwd_kernel, bs=bs),
        grid=(b, heads, nb),
        in_specs=specs(lambda i: i),
        out_specs=pl.BlockSpec((1, bs, dv), lambda bi, h, i: (bi, i, h)),
        out_shape=jax.ShapeDtypeStruct((b, s, dv_total), F32),
        scratch_shapes=scratch,
        compiler_params=params(
            vmem_limit_bytes=_vmem_limit(2 * (in_blocks + _nbytes((bs, dv), F32)) + scratch_bytes)),
        name="gla_forward_scan",
    )(p, p, p, r, wup_f, bias_f.reshape(1, dk_total))

    rev = lambda i: nb - 1 - i
    return pl.pallas_call(
        functools.partial(_gla_bwd_kernel, bs=bs),
        grid=(b, heads, nb),
        in_specs=specs(rev) + [
            pl.BlockSpec((1, bs, dv), lambda bi, h, i: (bi, rev(i), h)),
            pl.BlockSpec((1, bs, dv), lambda bi, h, i: (bi, rev(i), kg + h)),
            pl.BlockSpec((1, dv), lambda bi, h, i: (0, 0)),
        ],
        out_specs=pl.BlockSpec((1, bs, dv), lambda bi, h, i: (bi, rev(i), h)),
        out_shape=jax.ShapeDtypeStruct((b, s, dv_total), BF16),
        scratch_shapes=scratch,
        compiler_params=params(
            vmem_limit_bytes=_vmem_limit(2 * (in_blocks + 2 * _nbytes((bs, dv), F32) + _nbytes((bs, dv), BF16))
                                         + scratch_bytes)),
        name="gla_backward_scan",
    )(p, p, p, r, wup_b, bias_b.reshape(1, dk_total), o_fwd, p, g_norm.reshape(1, dv))


def gla_layer(h, g_mix, w_in, w_up_f, b_up_f, w_up_b, b_up_b, g_norm, w_out):
    b, s, d = h.shape
    t = b * s
    dk_total = w_up_f.shape[1]
    main = w_in.shape[1] - 2 * GATE_RANK
    h2 = h.reshape(t, d)
    hn = rmsnorm(h2, g_mix, BF16)
    p = matmul(hn, w_in[:, :main].astype(BF16), F32, name="gla_in_proj")
    w_r = jnp.pad(w_in[:, main:], ((0, 0), (0, LANES - 2 * GATE_RANK))).astype(BF16)
    r = matmul(hn, w_r, F32, name="gla_gate_down_proj")
    pad_f = jnp.pad(w_up_f, ((0, LANES - GATE_RANK), (0, 0))).astype(BF16)
    pad_b = jnp.pad(w_up_b, ((GATE_RANK, LANES - 2 * GATE_RANK), (0, 0))).astype(BF16)
    og = gla_core(p.reshape(b, s, main), r.reshape(b, s, LANES), pad_f, b_up_f, pad_b, b_up_b, g_norm,
                  heads=GLA_HEADS)
    out = matmul(og.reshape(t, -1), w_out.astype(BF16), F32, residual=h2, name="gla_out_proj")
    return out.reshape(b, s, d)


def mlp_layer(h, hn, w1, w2):
    a = matmul(hn, w1.astype(BF16), BF16, relu2=True, name="mlp_up_proj")
    return matmul_ktiled(a, w2.astype(BF16), h, name="mlp_down_proj")


def kernel(x, norm_mix, norm_mlp, norm_final, pool_w, pool_scale, gla_w_in, gla_w_up_f, gla_b_up_f, gla_w_up_b,
           gla_b_up_b, gla_g_norm, gla_w_out, mlp_w_in, mlp_w_out):
    b, s, d = x.shape
    t = b * s
    h, hn = pool_layer(x, norm_mix[0], norm_mlp[0], pool_w[0], pool_scale[0])
    h = mlp_layer(h.reshape(t, d), hn.reshape(t, d), mlp_w_in[0], mlp_w_out[0])
    h = gla_layer(h.reshape(b, s, d), norm_mix[1], gla_w_in[0], gla_w_up_f[0], gla_b_up_f[0], gla_w_up_b[0],
                  gla_b_up_b[0], gla_g_norm[0], gla_w_out[0]).reshape(t, d)
    hn = rmsnorm(h, norm_mlp[1], BF16)
    h = mlp_layer(h, hn, mlp_w_in[1], mlp_w_out[1])
    return rmsnorm(h, norm_final, F32).reshape(b, s, d)
```

```python
import functools

import jax
import jax.numpy as jnp
from jax import lax
from jax.experimental import pallas as pl
from jax.experimental.pallas import tpu as pltpu

EPS = 1e-6
POOL_WINDOWS = (2, 4, 8, 16)
GLA_HEADS = 4
GATE_RANK = 16
GATE_TAU = 16.0
CHUNK = 64

V7X_VMEM_BYTES = 64 * 1024 * 1024
LANES = 128
SUBLANES = 8

F32 = jnp.float32
BF16 = jnp.bfloat16


def _vmem_limit(block_bytes):
    return int(min(V7X_VMEM_BYTES - 4 * 1024 * 1024, 2 * block_bytes))


def _nbytes(shape, dtype):
    n = 1
    for s in shape:
        n *= s
    return n * jnp.dtype(dtype).itemsize


def _rmsnorm_rows(x, g):
    y = x * lax.rsqrt(jnp.mean(x * x, axis=-1, keepdims=True) + EPS)
    return y * g


def _rmsnorm_kernel(x_ref, g_ref, o_ref):
    o_ref[...] = _rmsnorm_rows(x_ref[...], g_ref[...]).astype(o_ref.dtype)


def rmsnorm(x, g, out_dtype, tm=512):
    m, d = x.shape
    blocks = 2 * (_nbytes((tm, d), x.dtype) + _nbytes((tm, d), out_dtype))
    return pl.pallas_call(
        _rmsnorm_kernel,
        grid=(m // tm,),
        in_specs=[pl.BlockSpec((tm, d), lambda i: (i, 0)), pl.BlockSpec((1, d), lambda i: (0, 0))],
        out_specs=pl.BlockSpec((tm, d), lambda i: (i, 0)),
        out_shape=jax.ShapeDtypeStruct((m, d), out_dtype),
        compiler_params=pltpu.CompilerParams(
            dimension_semantics=("arbitrary",), vmem_limit_bytes=_vmem_limit(blocks)),
        name="rmsnorm",
    )(x, g.reshape(1, d))


def _mm_kernel(x_ref, w_ref, *rest, relu2, residual):
    o_ref = rest[-1]
    acc = jnp.dot(x_ref[...], w_ref[...], preferred_element_type=F32)
    if relu2:
        acc = jnp.maximum(acc, 0.0)
        acc = acc * acc
    if residual:
        acc = rest[0][...] + acc
    o_ref[...] = acc.astype(o_ref.dtype)


def matmul(x, w, layer, out_dtype, *, n=None, relu2=False, residual=None, tm=1024, tn=1024, name="matmul"):
    m, k = x.shape
    n = w.shape[2] if n is None else n
    tn = min(tn, n)
    assert m % tm == 0 and n % tn == 0
    in_specs = [pl.BlockSpec((tm, k), lambda i, j: (i, 0)), pl.BlockSpec((None, k, tn), lambda i, j: (layer, 0, j))]
    args = [x, w]
    blocks = _nbytes((tm, k), x.dtype) + _nbytes((k, tn), w.dtype) + _nbytes((tm, tn), out_dtype)
    if residual is not None:
        in_specs.append(pl.BlockSpec((tm, tn), lambda i, j: (i, j)))
        args.append(residual)
        blocks += _nbytes((tm, tn), residual.dtype)
    return pl.pallas_call(
        functools.partial(_mm_kernel, relu2=relu2, residual=residual is not None),
        grid=(m // tm, n // tn),
        in_specs=in_specs,
        out_specs=pl.BlockSpec((tm, tn), lambda i, j: (i, j)),
        out_shape=jax.ShapeDtypeStruct((m, n), out_dtype),
        compiler_params=pltpu.CompilerParams(
            dimension_semantics=("arbitrary", "arbitrary"),
            vmem_limit_bytes=_vmem_limit(2 * blocks + _nbytes((tm, tn), F32))),
        name=name,
    )(*args)


def _mm_ktiled_kernel(x_ref, w_ref, res_ref, o_ref):
    kk = pl.program_id(2)
    part = jnp.dot(x_ref[...], w_ref[...], preferred_element_type=F32)

    @pl.when(kk == 0)
    def _():
        o_ref[...] = res_ref[...] + part

    @pl.when(kk > 0)
    def _():
        o_ref[...] += part


def matmul_ktiled(x, w, layer, residual, *, tm=1024, tn=1024, tk=4096, name="matmul_ktiled"):
    m, k = x.shape
    n = w.shape[2]
    assert m % tm == 0 and n % tn == 0 and k % tk == 0
    blocks = (_nbytes((tm, tk), x.dtype) + _nbytes((tk, tn), w.dtype) + 2 * _nbytes((tm, tn), F32))
    return pl.pallas_call(
        _mm_ktiled_kernel,
        grid=(m // tm, n // tn, k // tk),
        in_specs=[
            pl.BlockSpec((tm, tk), lambda i, j, kk: (i, kk)),
            pl.BlockSpec((None, tk, tn), lambda i, j, kk: (layer, kk, j)),
            pl.BlockSpec((tm, tn), lambda i, j, kk: (i, j)),
        ],
        out_specs=pl.BlockSpec((tm, tn), lambda i, j, kk: (i, j)),
        out_shape=jax.ShapeDtypeStruct((m, n), F32),
        compiler_params=pltpu.CompilerParams(
            dimension_semantics=("arbitrary", "arbitrary", "arbitrary"),
            vmem_limit_bytes=_vmem_limit(2 * blocks + _nbytes((tm, tn), F32))),
        name=name,
    )(x, w, residual)


def _pool_kernel(x_ref, xp_ref, xn_ref, gmix_ref, gmlp_ref, w_ref, scale_ref, h_ref, hn_ref, buf_ref,
                 *, tm, seq, halo):
    i = pl.program_id(1)
    nblk = pl.num_programs(1)
    gmix = gmix_ref[...]
    x = x_ref[0]
    d_model = x.shape[-1]
    group = d_model // len(POOL_WINDOWS)

    buf_ref[halo:halo + tm, :] = _rmsnorm_rows(x, gmix)
    buf_ref[0:halo, :] = jnp.where(i > 0, _rmsnorm_rows(xp_ref[0], gmix), 0.0)
    buf_ref[halo + tm:halo + tm + halo, :] = jnp.where(i < nblk - 1, _rmsnorm_rows(xn_ref[0], gmix), 0.0)

    pos = i * tm + lax.broadcasted_iota(jnp.int32, (tm, 1), 0)
    for gi, win in enumerate(POOL_WINDOWS):
        cols = slice(gi * group, (gi + 1) * group)
        half = win // 2
        acc = buf_ref[halo - half:halo - half + tm, cols]
        for off in range(-half + 1, half):
            acc = acc + buf_ref[halo + off:halo + off + tm, cols]
        count = (jnp.minimum(pos + half, seq) - jnp.maximum(pos - half, 0)).astype(F32)
        diff = acc / count - buf_ref[halo:halo + tm, cols]
        y = jnp.dot(diff.astype(BF16), w_ref[gi], preferred_element_type=F32)
        h_ref[0, :, cols] = x[:, cols] + y * scale_ref[:, cols]

    hn_ref[0] = _rmsnorm_rows(h_ref[0], gmlp_ref[...]).astype(hn_ref.dtype)


def pool_layer(x, g_mix, g_mlp, pool_w, pool_scale, tm=256):
    b, s, d = x.shape
    halo = SUBLANES
    assert max(POOL_WINDOWS) // 2 <= halo and tm % halo == 0 and s % tm == 0
    ng, gd, _ = pool_w.shape
    hb = tm // halo
    blocks = (2 * _nbytes((tm, d), F32) * 2 + 2 * _nbytes((tm, d), BF16) + 2 * _nbytes(pool_w.shape, BF16)
              + _nbytes((tm + 2 * halo, d), F32))
    return pl.pallas_call(
        functools.partial(_pool_kernel, tm=tm, seq=s, halo=halo),
        grid=(b, s // tm),
        in_specs=[
            pl.BlockSpec((1, tm, d), lambda bi, i: (bi, i, 0)),
            pl.BlockSpec((1, halo, d), lambda bi, i: (bi, jnp.maximum(i * hb - 1, 0), 0)),
            pl.BlockSpec((1, halo, d), lambda bi, i: (bi, jnp.minimum((i + 1) * hb, s // halo - 1), 0)),
            pl.BlockSpec((1, d), lambda bi, i: (0, 0)),
            pl.BlockSpec((1, d), lambda bi, i: (0, 0)),
            pl.BlockSpec((ng, gd, gd), lambda bi, i: (0, 0, 0)),
            pl.BlockSpec((1, d), lambda bi, i: (0, 0)),
        ],
        out_specs=[
            pl.BlockSpec((1, tm, d), lambda bi, i: (bi, i, 0)),
            pl.BlockSpec((1, tm, d), lambda bi, i: (bi, i, 0)),
        ],
        out_shape=[jax.ShapeDtypeStruct((b, s, d), F32), jax.ShapeDtypeStruct((b, s, d), BF16)],
        scratch_shapes=[pltpu.VMEM((tm + 2 * halo, d), F32)],
        compiler_params=pltpu.CompilerParams(
            dimension_semantics=("arbitrary", "arbitrary"), vmem_limit_bytes=_vmem_limit(blocks)),
        name="pool_layer",
    )(x, x, x, g_mix.reshape(1, d), g_mlp.reshape(1, d), pool_w.astype(BF16), pool_scale.reshape(1, d))


def _gla_block(q_ref, k_ref, v_ref, r_ref, wup_ref, bias_ref, state_ref, *, reverse):
    rows, dk = q_ref.shape[1], q_ref.shape[2]
    c = CHUNK
    ns = rows // c
    order = tuple(reversed(range(ns))) if reverse else tuple(range(ns))

    def load(ref):
        return jnp.concatenate([ref[0, ci * c:(ci + 1) * c, :] for ci in order], axis=0)

    q = load(q_ref) * (dk ** -0.5)
    k = load(k_ref)
    v = load(v_ref).astype(BF16)
    z = jnp.dot(load(r_ref).astype(BF16), wup_ref[0], preferred_element_type=F32)
    g = jax.nn.log_sigmoid(z + bias_ref[0]) * (1.0 / GATE_TAU)

    row = lax.broadcasted_iota(jnp.int32, (rows, rows), 0)
    col = lax.broadcasted_iota(jnp.int32, (rows, rows), 1)
    shift = c.bit_length() - 1
    row_chunk = lax.shift_right_logical(row, shift)
    col_chunk = lax.shift_right_logical(col, shift)
    causal = (col >= row) if reverse else (col <= row)
    same_chunk = (row_chunk == col_chunk) & causal
    tri = jnp.where(same_chunk, 1.0, 0.0).astype(BF16)
    g_hi = g.astype(BF16)
    g_lo = (g - g_hi.astype(F32)).astype(BF16)
    b = jnp.dot(tri, g_hi, preferred_element_type=F32) + jnp.dot(tri, g_lo, preferred_element_type=F32)

    last_i, mid_i = (0, c - c // 2) if reverse else (c - 1, c // 2 - 1)
    b_last = [b[p * c + last_i:p * c + last_i + 1, :] for p in range(ns)]
    b_mid = [b[p * c + mid_i:p * c + mid_i + 1, :] for p in range(ns)]
    a = [jnp.exp(x) for x in b_last]

    def prod(factors):
        out = None
        for f in factors:
            out = f if out is None else out * f
        return out

    q_intra, k_intra, q_inter, k_state, q_prev, k_next = [], [], [], [], [], []
    q_skip = {d: [] for d in range(2, ns)}
    for p in range(ns):
        rs = slice(p * c, (p + 1) * c)
        bp, qp, kp = b[rs], q[rs], k[rs]
        qn = qp * jnp.exp(bp)
        ks = kp * jnp.exp(b_last[p] - bp)
        q_intra.append((qp * jnp.exp(bp - b_mid[p])).astype(BF16))
        k_intra.append((kp * jnp.exp(b_mid[p] - bp)).astype(BF16))
        q_inter.append(qn.astype(BF16))
        k_state.append(ks.astype(BF16))
        before, after = prod(a[:p]), prod(a[p + 1:])
        q_prev.append((qn if before is None else qn * before).astype(BF16))
        k_next.append((ks if after is None else ks * after).astype(BF16))
        for d in range(2, p + 1):
            q_skip[d].append((qn * prod(a[p - d + 1:p])).astype(BF16))

    nt = (((1,), (1,)), ((), ()))
    s_intra = lax.dot_general(jnp.concatenate(q_intra, axis=0), jnp.concatenate(k_intra, axis=0), nt,
                              preferred_element_type=F32)
    cross_lhs = q_inter + [x for d in range(2, ns) for x in q_skip[d]]
    s_cross = lax.dot_general(jnp.concatenate(cross_lhs, axis=0), jnp.concatenate(k_state, axis=0), nt,
                              preferred_element_type=F32)

    skip_base = {}
    base = ns * c
    for d in range(2, ns):
        skip_base[d] = base
        base += (ns - d) * c
    score_rows = []
    row_in = lax.broadcasted_iota(jnp.int32, (c, rows), 0)
    col_in = lax.broadcasted_iota(jnp.int32, (c, rows), 1)
    col_chunk_in = lax.shift_right_logical(col_in, shift)
    for p in range(ns):
        rs = slice(p * c, (p + 1) * c)
        col_local = col_in - p * c
        causal_in = (col_local >= row_in) if reverse else (col_local <= row_in)
        sc = jnp.where((col_chunk_in == p) & causal_in, s_intra[rs], 0.0)
        if p >= 1:
            sc = jnp.where(col_chunk_in == p - 1, s_cross[rs], sc)
        for d in range(2, p + 1):
            off = skip_base[d] + (p - d) * c
            sc = jnp.where(col_chunk_in == p - d, s_cross[off:off + c], sc)
        score_rows.append(sc.astype(BF16))
    scores = jnp.concatenate(score_rows, axis=0)

    state = state_ref[...]
    o = jnp.dot(scores, v, preferred_element_type=F32)
    o = o + jnp.dot(jnp.concatenate(q_prev, axis=0), state.astype(BF16), preferred_element_type=F32)

    upd = lax.dot_general(jnp.concatenate(k_next, axis=0), v, (((0,), (0,)), ((), ())),
                          preferred_element_type=F32)
    decay_t = jnp.transpose(jnp.broadcast_to(prod(a), (LANES, dk)))
    decay = jnp.concatenate([decay_t] * (state.shape[1] // LANES), axis=1)
    state_ref[...] = decay * state + upd
    return [o[p * c:(p + 1) * c] for p in range(ns)], order


def _gla_kernel(q_ref, k_ref, v_ref, r_ref, wup_ref, bias_ref, gate_ref, gnorm_ref, o_ref, state_ref, acc_ref,
                *, nb):
    s = pl.program_id(2)
    rows = q_ref.shape[1]
    c = CHUNK

    @pl.when((s == 0) | (s == nb))
    def _():
        state_ref[...] = jnp.zeros_like(state_ref)

    @pl.when(s < nb)
    def _():
        pieces, order = _gla_block(q_ref, k_ref, v_ref, r_ref, wup_ref, bias_ref, state_ref, reverse=False)
        base = pl.multiple_of(s * rows, rows)
        for piece, ci in zip(pieces, order):
            acc_ref[pl.ds(base + ci * c, c), :] = piece

    @pl.when(s >= nb)
    def _():
        pieces, order = _gla_block(q_ref, k_ref, v_ref, r_ref, wup_ref, bias_ref, state_ref, reverse=True)
        base = pl.multiple_of((2 * nb - 1 - s) * rows, rows)
        gnorm = gnorm_ref[...]
        for piece, ci in zip(pieces, order):
            o = _rmsnorm_rows(acc_ref[pl.ds(base + ci * c, c), :] + piece, gnorm)
            gate = gate_ref[0, ci * c:(ci + 1) * c, :]
            o_ref[0, ci * c:(ci + 1) * c, :] = (o * jax.nn.silu(gate)).astype(o_ref.dtype)


def gla_core(p, r, wup, bias, g_norm, *, heads, bs=256):
    b, s, width = p.shape
    dv_total = width // 3
    dk_total = dv_total // 2
    dk, dv = dk_total // heads, dv_total // heads
    nb = s // bs
    kq = dk_total // dk
    kv = (2 * dk_total) // dv
    kg = (2 * dk_total + dv_total) // dv
    rw = r.shape[-1]

    def blk(i):
        return jnp.where(i < nb, i, 2 * nb - 1 - i)

    def out_blk(i):
        return jnp.where(i < nb, nb - 1, 2 * nb - 1 - i)

    blocks = (2 * _nbytes((bs, dk), F32) + 2 * _nbytes((bs, dv), F32) + _nbytes((bs, rw), F32)
              + _nbytes((bs, dv), BF16))
    scratch_bytes = _nbytes((dk, dv), F32) + _nbytes((s, dv), F32)
    return pl.pallas_call(
        functools.partial(_gla_kernel, nb=nb),
        grid=(b, heads, 2 * nb),
        in_specs=[
            pl.BlockSpec((1, bs, dk), lambda bi, h, i: (bi, blk(i), h)),
            pl.BlockSpec((1, bs, dk), lambda bi, h, i: (bi, blk(i), kq + h)),
            pl.BlockSpec((1, bs, dv), lambda bi, h, i: (bi, blk(i), kv + h)),
            pl.BlockSpec((1, bs, rw), lambda bi, h, i: (bi, blk(i), 0)),
            pl.BlockSpec((1, rw, dk), lambda bi, h, i: (i // nb, 0, h)),
            pl.BlockSpec((1, 1, dk), lambda bi, h, i: (i // nb, 0, h)),
            pl.BlockSpec((1, bs, dv), lambda bi, h, i: (bi, out_blk(i), kg + h)),
            pl.BlockSpec((1, dv), lambda bi, h, i: (0, 0)),
        ],
        out_specs=pl.BlockSpec((1, bs, dv), lambda bi, h, i: (bi, out_blk(i), h)),
        out_shape=jax.ShapeDtypeStruct((b, s, dv_total), BF16),
        scratch_shapes=[pltpu.VMEM((dk, dv), F32), pltpu.VMEM((s, dv), F32)],
        compiler_params=pltpu.CompilerParams(
            dimension_semantics=("arbitrary", "arbitrary", "arbitrary"),
            vmem_limit_bytes=_vmem_limit(2 * blocks + scratch_bytes)),
        name="gla_scan",
    )(p, p, p, r, wup, bias, p, g_norm.reshape(1, dv))


def gla_layer(h, g_mix, w_in, layer, w_up_f, b_up_f, w_up_b, b_up_b, g_norm, w_out):
    b, s, d = h.shape
    t = b * s
    main = w_in.shape[2] - 2 * GATE_RANK
    h2 = h.reshape(t, d)
    hn = rmsnorm(h2, g_mix, BF16)
    p = matmul(hn, w_in, layer, F32, n=main, name="gla_in_proj")
    w_r = jnp.pad(w_in[layer, :, main:], ((0, 0), (0, LANES - 2 * GATE_RANK)))[None]
    r = matmul(hn, w_r, 0, F32, name="gla_gate_down_proj")
    wup = jnp.stack([jnp.pad(w_up_f, ((0, LANES - GATE_RANK), (0, 0))),
                     jnp.pad(w_up_b, ((GATE_RANK, LANES - 2 * GATE_RANK), (0, 0)))]).astype(BF16)
    bias = jnp.stack([b_up_f, b_up_b])[:, None, :]
    og = gla_core(p.reshape(b, s, main), r.reshape(b, s, LANES), wup, bias, g_norm, heads=GLA_HEADS)
    out = matmul(og.reshape(t, -1), w_out, layer, F32, residual=h2, name="gla_out_proj")
    return out.reshape(b, s, d)


def mlp_layer(h, hn, w1, w2, layer):
    a = matmul(hn, w1, layer, BF16, relu2=True, name="mlp_up_proj")
    return matmul_ktiled(a, w2, layer, h, name="mlp_down_proj")


def kernel(x, norm_mix, norm_mlp, norm_final, pool_w, pool_scale, gla_w_in, gla_w_up_f, gla_b_up_f, gla_w_up_b,
           gla_b_up_b, gla_g_norm, gla_w_out, mlp_w_in, mlp_w_out):
    b, s, d = x.shape
    t = b * s
    w1, w2 = mlp_w_in.astype(BF16), mlp_w_out.astype(BF16)
    w_in, w_out = gla_w_in.astype(BF16), gla_w_out.astype(BF16)
    h, hn = pool_layer(x, norm_mix[0], norm_mlp[0], pool_w[0], pool_scale[0])
    h = mlp_layer(h.reshape(t, d), hn.reshape(t, d), w1, w2, 0)
    h = gla_layer(h.reshape(b, s, d), norm_mix[1], w_in, 0, gla_w_up_f[0], gla_b_up_f[0], gla_w_up_b[0],
                  gla_b_up_b[0], gla_g_norm[0], w_out).reshape(t, d)
    hn = rmsnorm(h, norm_mlp[1], BF16)
    h = mlp_layer(h, hn, w1, w2, 1)
    return rmsnorm(h, norm_final, F32).reshape(b, s, d)
```

```python
import functools
import math
from typing import NamedTuple

import jax
import jax.numpy as jnp
from jax import lax
from jax.experimental import pallas as pl
from jax.experimental.pallas import tpu as pltpu

EPS = 1e-6
POOL_WINDOWS = (2, 4, 8, 16)
GLA_HEADS = 4
GATE_RANK = 16
GATE_TAU = 16.0
CHUNK = 64

V7X_VMEM_BYTES = 64 * 1024 * 1024
LANES = 128
SUBLANES = 8

F32 = jnp.float32
BF16 = jnp.bfloat16


def _vmem_limit(block_bytes):
    return int(min(V7X_VMEM_BYTES - 4 * 1024 * 1024, 2 * block_bytes))


def _nbytes(shape, dtype):
    n = 1
    for s in shape:
        n *= s
    return n * jnp.dtype(dtype).itemsize


def _rmsnorm_rows(x, g):
    y = x * lax.rsqrt(jnp.mean(x * x, axis=-1, keepdims=True) + EPS)
    return y * g


def _rmsnorm_kernel(x_ref, g_ref, o_ref):
    o_ref[...] = _rmsnorm_rows(x_ref[...], g_ref[...]).astype(o_ref.dtype)


def rmsnorm(x, g, out_dtype, tm=512):
    m, d = x.shape
    blocks = 2 * (_nbytes((tm, d), x.dtype) + _nbytes((tm, d), out_dtype))
    return pl.pallas_call(
        _rmsnorm_kernel,
        grid=(m // tm,),
        in_specs=[pl.BlockSpec((tm, d), lambda i: (i, 0)), pl.BlockSpec((1, d), lambda i: (0, 0))],
        out_specs=pl.BlockSpec((tm, d), lambda i: (i, 0)),
        out_shape=jax.ShapeDtypeStruct((m, d), out_dtype),
        compiler_params=pltpu.CompilerParams(
            dimension_semantics=("arbitrary",), vmem_limit_bytes=_vmem_limit(blocks)),
        name="rmsnorm",
    )(x, g.reshape(1, d))


class CastJob(NamedTuple):
    src: jax.Array
    layer: int
    period: int = 1

    def specs(self, grid):
        _, r, c = self.src.shape
        steps = 1
        for g in grid:
            steps *= g
        assert steps % self.period == 0 and r % (steps // self.period) == 0
        slab = r // (steps // self.period)
        assert slab % (2 * SUBLANES) == 0

        def slab_of(*idx):
            step = 0
            for g, i in zip(grid, idx):
                step = step * g + i
            return step // self.period

        layer = self.layer
        return (pl.BlockSpec((None, slab, c), lambda *idx: (layer, slab_of(*idx), 0)),
                pl.BlockSpec((slab, c), lambda *idx: (slab_of(*idx), 0)),
                jax.ShapeDtypeStruct((r, c), BF16),
                _nbytes((slab, c), F32) + _nbytes((slab, c), BF16))


def _run_casts(src_refs, dst_refs):
    for src, dst in zip(src_refs, dst_refs):
        dst[...] = src[...].astype(dst.dtype)


def _mm_kernel(*refs, relu2, residual, ncast):
    n_in = 2 + int(residual)
    x_ref, w_ref = refs[0], refs[1]
    o_ref = refs[n_in + ncast]
    acc = jnp.dot(x_ref[...], w_ref[...], preferred_element_type=F32)
    if relu2:
        acc = jnp.maximum(acc, 0.0)
        acc = acc * acc
    if residual:
        acc = refs[2][...] + acc
    o_ref[...] = acc.astype(o_ref.dtype)
    _run_casts(refs[n_in:n_in + ncast], refs[n_in + ncast + 1:])


def matmul(x, w, layer, out_dtype, *, n=None, relu2=False, residual=None, casts=(), tm=1024, tn=1024,
           name="matmul"):
    m, k = x.shape
    n = w.shape[2] if n is None else n
    tn = min(tn, n)
    assert m % tm == 0 and n % tn == 0
    grid = (m // tm, n // tn)
    in_specs = [pl.BlockSpec((tm, k), lambda i, j: (i, 0)), pl.BlockSpec((None, k, tn), lambda i, j: (layer, 0, j))]
    args = [x, w]
    blocks = _nbytes((tm, k), x.dtype) + _nbytes((k, tn), w.dtype) + _nbytes((tm, tn), out_dtype)
    if residual is not None:
        in_specs.append(pl.BlockSpec((tm, tn), lambda i, j: (i, j)))
        args.append(residual)
        blocks += _nbytes((tm, tn), residual.dtype)
    out_specs = [pl.BlockSpec((tm, tn), lambda i, j: (i, j))]
    out_shape = [jax.ShapeDtypeStruct((m, n), out_dtype)]
    for job in casts:
        cin, cout, cshape, cbytes = job.specs(grid)
        in_specs.append(cin)
        args.append(job.src)
        out_specs.append(cout)
        out_shape.append(cshape)
        blocks += cbytes
    outs = pl.pallas_call(
        functools.partial(_mm_kernel, relu2=relu2, residual=residual is not None, ncast=len(casts)),
        grid=grid,
        in_specs=in_specs,
        out_specs=out_specs,
        out_shape=out_shape,
        compiler_params=pltpu.CompilerParams(
            dimension_semantics=("arbitrary", "arbitrary"),
            vmem_limit_bytes=_vmem_limit(2 * blocks + _nbytes((tm, tn), F32))),
        name=name,
    )(*args)
    return outs if casts else outs[0]


def _mm_ktiled_kernel(x_ref, w_ref, res_ref, o_ref):
    kk = pl.program_id(2)
    part = jnp.dot(x_ref[...], w_ref[...], preferred_element_type=F32)

    @pl.when(kk == 0)
    def _():
        o_ref[...] = res_ref[...] + part

    @pl.when(kk > 0)
    def _():
        o_ref[...] += part


def matmul_ktiled(x, w, layer, residual, *, tm=1024, tn=1024, tk=4096, name="matmul_ktiled"):
    m, k = x.shape
    n = w.shape[2]
    assert m % tm == 0 and n % tn == 0 and k % tk == 0
    blocks = (_nbytes((tm, tk), x.dtype) + _nbytes((tk, tn), w.dtype) + 2 * _nbytes((tm, tn), F32))
    return pl.pallas_call(
        _mm_ktiled_kernel,
        grid=(m // tm, n // tn, k // tk),
        in_specs=[
            pl.BlockSpec((tm, tk), lambda i, j, kk: (i, kk)),
            pl.BlockSpec((None, tk, tn), lambda i, j, kk: (layer, kk, j)),
            pl.BlockSpec((tm, tn), lambda i, j, kk: (i, j)),
        ],
        out_specs=pl.BlockSpec((tm, tn), lambda i, j, kk: (i, j)),
        out_shape=jax.ShapeDtypeStruct((m, n), F32),
        compiler_params=pltpu.CompilerParams(
            dimension_semantics=("arbitrary", "arbitrary", "arbitrary"),
            vmem_limit_bytes=_vmem_limit(2 * blocks + _nbytes((tm, tn), F32))),
        name=name,
    )(x, w, residual)


def _pool_kernel(x_ref, xp_ref, xn_ref, gmix_ref, gmlp_ref, w_ref, scale_ref, h_ref, hn_ref, buf_ref,
                 *, tm, seq, halo):
    i = pl.program_id(1)
    nblk = pl.num_programs(1)
    gmix = gmix_ref[...]
    x = x_ref[0]
    d_model = x.shape[-1]
    group = d_model // len(POOL_WINDOWS)

    buf_ref[halo:halo + tm, :] = _rmsnorm_rows(x, gmix)
    buf_ref[0:halo, :] = jnp.where(i > 0, _rmsnorm_rows(xp_ref[0], gmix), 0.0)
    buf_ref[halo + tm:halo + tm + halo, :] = jnp.where(i < nblk - 1, _rmsnorm_rows(xn_ref[0], gmix), 0.0)

    pos = i * tm + lax.broadcasted_iota(jnp.int32, (tm, 1), 0)
    for gi, win in enumerate(POOL_WINDOWS):
        cols = slice(gi * group, (gi + 1) * group)
        half = win // 2
        acc = buf_ref[halo - half:halo - half + tm, cols]
        for off in range(-half + 1, half):
            acc = acc + buf_ref[halo + off:halo + off + tm, cols]
        count = (jnp.minimum(pos + half, seq) - jnp.maximum(pos - half, 0)).astype(F32)
        diff = acc / count - buf_ref[halo:halo + tm, cols]
        y = jnp.dot(diff.astype(BF16), w_ref[gi], preferred_element_type=F32)
        h_ref[0, :, cols] = x[:, cols] + y * scale_ref[:, cols]

    hn_ref[0] = _rmsnorm_rows(h_ref[0], gmlp_ref[...]).astype(hn_ref.dtype)


def pool_layer(x, g_mix, g_mlp, pool_w, pool_scale, tm=256):
    b, s, d = x.shape
    halo = SUBLANES
    assert max(POOL_WINDOWS) // 2 <= halo and tm % halo == 0 and s % tm == 0
    ng, gd, _ = pool_w.shape
    hb = tm // halo
    blocks = (2 * _nbytes((tm, d), F32) * 2 + 2 * _nbytes((tm, d), BF16) + 2 * _nbytes(pool_w.shape, BF16)
              + _nbytes((tm + 2 * halo, d), F32))
    return pl.pallas_call(
        functools.partial(_pool_kernel, tm=tm, seq=s, halo=halo),
        grid=(b, s // tm),
        in_specs=[
            pl.BlockSpec((1, tm, d), lambda bi, i: (bi, i, 0)),
            pl.BlockSpec((1, halo, d), lambda bi, i: (bi, jnp.maximum(i * hb - 1, 0), 0)),
            pl.BlockSpec((1, halo, d), lambda bi, i: (bi, jnp.minimum((i + 1) * hb, s // halo - 1), 0)),
            pl.BlockSpec((1, d), lambda bi, i: (0, 0)),
            pl.BlockSpec((1, d), lambda bi, i: (0, 0)),
            pl.BlockSpec((ng, gd, gd), lambda bi, i: (0, 0, 0)),
            pl.BlockSpec((1, d), lambda bi, i: (0, 0)),
        ],
        out_specs=[
            pl.BlockSpec((1, tm, d), lambda bi, i: (bi, i, 0)),
            pl.BlockSpec((1, tm, d), lambda bi, i: (bi, i, 0)),
        ],
        out_shape=[jax.ShapeDtypeStruct((b, s, d), F32), jax.ShapeDtypeStruct((b, s, d), BF16)],
        scratch_shapes=[pltpu.VMEM((tm + 2 * halo, d), F32)],
        compiler_params=pltpu.CompilerParams(
            dimension_semantics=("arbitrary", "arbitrary"), vmem_limit_bytes=_vmem_limit(blocks)),
        name="pool_layer",
    )(x, x, x, g_mix.reshape(1, d), g_mlp.reshape(1, d), pool_w.astype(BF16), pool_scale.reshape(1, d))


def _gla_block(q_ref, k_ref, v_ref, r_ref, wup_ref, bias_ref, state_ref, *, head, dk, dv, reverse):
    rows = q_ref.shape[1]
    c = CHUNK
    ns = rows // c
    order = tuple(reversed(range(ns))) if reverse else tuple(range(ns))
    kcols = slice(head * dk, (head + 1) * dk)
    vcols = slice(head * dv, (head + 1) * dv)

    def load(ref, cols):
        return jnp.concatenate([ref[0, ci * c:(ci + 1) * c, cols] for ci in order], axis=0)

    q = load(q_ref, kcols) * (dk ** -0.5)
    k = load(k_ref, kcols)
    v = load(v_ref, vcols).astype(BF16)
    z = jnp.dot(load(r_ref, slice(None)).astype(BF16), wup_ref[0, :, kcols], preferred_element_type=F32)
    g = jax.nn.log_sigmoid(z + bias_ref[0, :, kcols]) * (math.log2(math.e) / GATE_TAU)

    row = lax.broadcasted_iota(jnp.int32, (rows, rows), 0)
    col = lax.broadcasted_iota(jnp.int32, (rows, rows), 1)
    shift = c.bit_length() - 1
    row_chunk = lax.shift_right_logical(row, shift)
    col_chunk = lax.shift_right_logical(col, shift)
    causal = (col >= row) if reverse else (col <= row)
    same_chunk = (row_chunk == col_chunk) & causal
    tri = jnp.where(same_chunk, 1.0, 0.0).astype(BF16)
    g_hi = g.astype(BF16)
    g_lo = (g - g_hi.astype(F32)).astype(BF16)
    b = jnp.dot(tri, g_hi, preferred_element_type=F32) + jnp.dot(tri, g_lo, preferred_element_type=F32)

    last_i, mid_i = (0, c - c // 2) if reverse else (c - 1, c // 2 - 1)
    b_last = [b[p * c + last_i:p * c + last_i + 1, :] for p in range(ns)]
    b_mid = [b[p * c + mid_i:p * c + mid_i + 1, :] for p in range(ns)]
    a = [jnp.exp2(x) for x in b_last]

    def prod(factors):
        out = None
        for f in factors:
            out = f if out is None else out * f
        return out

    q_intra, k_intra, q_inter, k_state, q_prev, k_next = [], [], [], [], [], []
    q_skip = {d: [] for d in range(2, ns)}
    for p in range(ns):
        rs = slice(p * c, (p + 1) * c)
        bp, qp, kp = b[rs], q[rs], k[rs]
        qn = qp * jnp.exp2(bp)
        ks = kp * jnp.exp2(b_last[p] - bp)
        q_intra.append((qp * jnp.exp2(bp - b_mid[p])).astype(BF16))
        k_intra.append((kp * jnp.exp2(b_mid[p] - bp)).astype(BF16))
        q_inter.append(qn.astype(BF16))
        k_state.append(ks.astype(BF16))
        before, after = prod(a[:p]), prod(a[p + 1:])
        q_prev.append((qn if before is None else qn * before).astype(BF16))
        k_next.append((ks if after is None else ks * after).astype(BF16))
        for d in range(2, p + 1):
            q_skip[d].append((qn * prod(a[p - d + 1:p])).astype(BF16))

    nt = (((1,), (1,)), ((), ()))
    s_intra = lax.dot_general(jnp.concatenate(q_intra, axis=0), jnp.concatenate(k_intra, axis=0), nt,
                              preferred_element_type=F32)
    cross_lhs = q_inter + [x for d in range(2, ns) for x in q_skip[d]]
    s_cross = lax.dot_general(jnp.concatenate(cross_lhs, axis=0), jnp.concatenate(k_state, axis=0), nt,
                              preferred_element_type=F32)

    skip_base = {}
    base = ns * c
    for d in range(2, ns):
        skip_base[d] = base
        base += (ns - d) * c
    score_rows = []
    row_in = lax.broadcasted_iota(jnp.int32, (c, rows), 0)
    col_in = lax.broadcasted_iota(jnp.int32, (c, rows), 1)
    col_chunk_in = lax.shift_right_logical(col_in, shift)
    for p in range(ns):
        rs = slice(p * c, (p + 1) * c)
        col_local = col_in - p * c
        causal_in = (col_local >= row_in) if reverse else (col_local <= row_in)
        sc = jnp.where((col_chunk_in == p) & causal_in, s_intra[rs], 0.0)
        if p >= 1:
            sc = jnp.where(col_chunk_in == p - 1, s_cross[rs], sc)
        for d in range(2, p + 1):
            off = skip_base[d] + (p - d) * c
            sc = jnp.where(col_chunk_in == p - d, s_cross[off:off + c], sc)
        score_rows.append(sc.astype(BF16))
    scores = jnp.concatenate(score_rows, axis=0)

    state = state_ref[head]
    o = jnp.dot(scores, v, preferred_element_type=F32)
    o = o + jnp.dot(jnp.concatenate(q_prev, axis=0), state.astype(BF16), preferred_element_type=F32)

    upd = lax.dot_general(jnp.concatenate(k_next, axis=0), v, (((0,), (0,)), ((), ())),
                          preferred_element_type=F32)
    decay_t = jnp.transpose(jnp.broadcast_to(prod(a), (LANES, dk)))
    decay = jnp.concatenate([decay_t] * (dv // LANES), axis=1)
    state_ref[head] = decay * state + upd
    return [o[p * c:(p + 1) * c] for p in range(ns)], order


def _gla_kernel(*refs, nb, heads, dk, dv, ncast):
    q_ref, k_ref, v_ref, r_ref, wup_ref, bias_ref, gate_ref, gnorm_ref = refs[:8]
    o_ref = refs[8 + ncast]
    state_ref, acc_ref = refs[-2:]
    s = pl.program_id(2)
    rows = q_ref.shape[1]
    c = CHUNK
    block = functools.partial(_gla_block, q_ref, k_ref, v_ref, r_ref, wup_ref, bias_ref, state_ref, dk=dk, dv=dv)

    @pl.when((s == 0) | (s == nb))
    def _():
        state_ref[...] = jnp.zeros_like(state_ref)

    @pl.when(s < nb)
    def _():
        base = pl.multiple_of(s * rows, rows)
        for head in range(heads):
            pieces, order = block(head=head, reverse=False)
            for piece, ci in zip(pieces, order):
                acc_ref[pl.ds(base + ci * c, c), head * dv:(head + 1) * dv] = piece

    @pl.when(s >= nb)
    def _():
        base = pl.multiple_of((2 * nb - 1 - s) * rows, rows)
        gnorm = gnorm_ref[...]
        for head in range(heads):
            pieces, order = block(head=head, reverse=True)
            vcols = slice(head * dv, (head + 1) * dv)
            for piece, ci in zip(pieces, order):
                o = _rmsnorm_rows(acc_ref[pl.ds(base + ci * c, c), vcols] + piece, gnorm)
                gate = gate_ref[0, ci * c:(ci + 1) * c, vcols]
                o_ref[0, ci * c:(ci + 1) * c, vcols] = (o * jax.nn.silu(gate)).astype(o_ref.dtype)

    _run_casts(refs[8:8 + ncast], refs[8 + ncast + 1:-2])


def gla_core(p, r, wup, bias, g_norm, *, heads, casts=(), bs=256, heads_per_step=2):
    b, s, width = p.shape
    dv_total = width // 3
    dk_total = dv_total // 2
    dk, dv = dk_total // heads, dv_total // heads
    hp = heads_per_step
    assert heads % hp == 0 and s % bs == 0
    gk, gv = hp * dk, hp * dv
    nb = s // bs
    kq = dk_total // gk
    kv = (2 * dk_total) // gv
    kg = (2 * dk_total + dv_total) // gv
    rw = r.shape[-1]
    grid = (b, heads // hp, 2 * nb)

    def blk(i):
        return jnp.where(i < nb, i, 2 * nb - 1 - i)

    def out_blk(i):
        return jnp.where(i < nb, nb - 1, 2 * nb - 1 - i)

    in_specs = [
        pl.BlockSpec((1, bs, gk), lambda bi, h, i: (bi, blk(i), h)),
        pl.BlockSpec((1, bs, gk), lambda bi, h, i: (bi, blk(i), kq + h)),
        pl.BlockSpec((1, bs, gv), lambda bi, h, i: (bi, blk(i), kv + h)),
        pl.BlockSpec((1, bs, rw), lambda bi, h, i: (bi, blk(i), 0)),
        pl.BlockSpec((1, rw, gk), lambda bi, h, i: (i // nb, 0, h)),
        pl.BlockSpec((1, 1, gk), lambda bi, h, i: (i // nb, 0, h)),
        pl.BlockSpec((1, bs, gv), lambda bi, h, i: (bi, out_blk(i), kg + h)),
        pl.BlockSpec((1, dv), lambda bi, h, i: (0, 0)),
    ]
    args = [p, p, p, r, wup, bias, p, g_norm.reshape(1, dv)]
    out_specs = [pl.BlockSpec((1, bs, gv), lambda bi, h, i: (bi, out_blk(i), h))]
    out_shape = [jax.ShapeDtypeStruct((b, s, dv_total), BF16)]
    blocks = (2 * _nbytes((bs, gk), F32) + 2 * _nbytes((bs, gv), F32) + _nbytes((bs, rw), F32)
              + _nbytes((bs, gv), BF16))
    for job in casts:
        cin, cout, cshape, cbytes = job.specs(grid)
        in_specs.append(cin)
        args.append(job.src)
        out_specs.append(cout)
        out_shape.append(cshape)
        blocks += cbytes
    scratch_bytes = _nbytes((hp, dk, dv), F32) + _nbytes((s, gv), F32)
    outs = pl.pallas_call(
        functools.partial(_gla_kernel, nb=nb, heads=hp, dk=dk, dv=dv, ncast=len(casts)),
        grid=grid,
        in_specs=in_specs,
        out_specs=out_specs,
        out_shape=out_shape,
        scratch_shapes=[pltpu.VMEM((hp, dk, dv), F32), pltpu.VMEM((s, gv), F32)],
        compiler_params=pltpu.CompilerParams(
            dimension_semantics=("arbitrary", "arbitrary", "arbitrary"),
            vmem_limit_bytes=_vmem_limit(2 * blocks + scratch_bytes)),
        name="gla_scan",
    )(*args)
    return outs if casts else outs[0]


def gla_layer(h, g_mix, w_in, w_out_f32, layer, w_up_f, b_up_f, w_up_b, b_up_b, g_norm, next_cast):
    b, s, d = h.shape
    t = b * s
    main = w_in.shape[1] - 2 * GATE_RANK
    h2 = h.reshape(t, d)
    hn = rmsnorm(h2, g_mix, BF16)
    p, w_out = matmul(hn, w_in[None], 0, F32, n=main, casts=(CastJob(w_out_f32, layer, period=3),),
                      name="gla_in_proj")
    w_r = jnp.pad(w_in[:, main:], ((0, 0), (0, LANES - 2 * GATE_RANK)))[None]
    r = matmul(hn, w_r, 0, F32, name="gla_gate_down_proj")
    wup = jnp.stack([jnp.pad(w_up_f, ((0, LANES - GATE_RANK), (0, 0))),
                     jnp.pad(w_up_b, ((GATE_RANK, LANES - 2 * GATE_RANK), (0, 0)))]).astype(BF16)
    bias = jnp.stack([b_up_f, b_up_b])[:, None, :]
    og, next_bf16 = gla_core(p.reshape(b, s, main), r.reshape(b, s, LANES), wup, bias, g_norm, heads=GLA_HEADS,
                             casts=(next_cast,))
    out = matmul(og.reshape(t, -1), w_out[None], 0, F32, residual=h2, name="gla_out_proj")
    return out.reshape(b, s, d), next_bf16


def mlp_layer(h, hn, w1, w2_f32, layer, extra_casts=()):
    a, w2, *extra = matmul(hn, w1[None], 0, BF16, relu2=True, casts=(CastJob(w2_f32, layer),) + tuple(extra_casts),
                           name="mlp_up_proj")
    return matmul_ktiled(a, w2[None], 0, h, name="mlp_down_proj"), extra


def kernel(x, norm_mix, norm_mlp, norm_final, pool_w, pool_scale, gla_w_in, gla_w_up_f, gla_b_up_f, gla_w_up_b,
           gla_b_up_b, gla_g_norm, gla_w_out, mlp_w_in, mlp_w_out):
    b, s, d = x.shape
    t = b * s
    h, hn = pool_layer(x, norm_mix[0], norm_mlp[0], pool_w[0], pool_scale[0])
    h, (w_in,) = mlp_layer(h.reshape(t, d), hn.reshape(t, d), mlp_w_in[0].astype(BF16), mlp_w_out, 0,
                           extra_casts=(CastJob(gla_w_in, 0),))
    h, w1 = gla_layer(h.reshape(b, s, d), norm_mix[1], w_in, gla_w_out, 0, gla_w_up_f[0], gla_b_up_f[0],
                      gla_w_up_b[0], gla_b_up_b[0], gla_g_norm[0], CastJob(mlp_w_in, 1))
    h = h.reshape(t, d)
    hn = rmsnorm(h, norm_mlp[1], BF16)
    h, _ = mlp_layer(h, hn, w1, mlp_w_out, 1)
    return rmsnorm(h, norm_final, F32).reshape(b, s, d)
```

```python
import functools
import math
from typing import NamedTuple, Optional

import jax
import jax.numpy as jnp
from jax import lax
from jax.experimental import pallas as pl
from jax.experimental.pallas import tpu as pltpu

EPS = 1e-6
POOL_WINDOWS = (2, 4, 8, 16)
GLA_HEADS = 4
GATE_RANK = 16
GATE_TAU = 16.0
CHUNK = 64

V7X_VMEM_BYTES = 64 * 1024 * 1024
LANES = 128
SUBLANES = 8

F32 = jnp.float32
BF16 = jnp.bfloat16


def _vmem_limit(block_bytes):
    return int(min(V7X_VMEM_BYTES - 4 * 1024 * 1024, 2 * block_bytes))


def _nbytes(shape, dtype):
    n = 1
    for s in shape:
        n *= s
    return n * jnp.dtype(dtype).itemsize


def _lane_tile(x, n):
    return jnp.concatenate([x] * (n // LANES), axis=1)


def _row_sumsq(x):
    return jnp.broadcast_to(jnp.sum(x * x, axis=-1, keepdims=True), (x.shape[0], LANES))


def _rmsnorm_rows(x, g):
    y = x * lax.rsqrt(jnp.mean(x * x, axis=-1, keepdims=True) + EPS)
    return y * g


class CastJob(NamedTuple):
    src: jax.Array
    layer: int
    period: int = 1
    gain: Optional[jax.Array] = None


class _CastPlan(NamedTuple):
    in_specs: list
    args: list
    out_spec: pl.BlockSpec
    out_shape: jax.ShapeDtypeStruct
    nbytes: int


def _plan_cast(job, grid):
    _, r, c = job.src.shape
    steps = 1
    for g in grid:
        steps *= g
    assert steps % job.period == 0 and r % (steps // job.period) == 0
    slab = r // (steps // job.period)
    assert slab % (2 * SUBLANES) == 0

    def slab_of(*idx):
        step = 0
        for g, i in zip(grid, idx):
            step = step * g + i
        return step // job.period

    layer = job.layer
    in_specs = [pl.BlockSpec((None, slab, c), lambda *idx: (layer, slab_of(*idx), 0))]
    args = [job.src]
    if job.gain is not None:
        in_specs.append(pl.BlockSpec((slab, 1), lambda *idx: (slab_of(*idx), 0)))
        args.append(job.gain.reshape(r, 1))
    return _CastPlan(in_specs, args, pl.BlockSpec((slab, c), lambda *idx: (slab_of(*idx), 0)),
                     jax.ShapeDtypeStruct((r, c), BF16), _nbytes((slab, c), F32) + _nbytes((slab, c), BF16))


def _take_cast_refs(it, gains):
    return [(next(it), next(it) if g else None) for g in gains]


def _run_casts(cast_in, cast_out):
    for (src, gain), dst in zip(cast_in, cast_out):
        val = src[...] if gain is None else src[...] * gain[...]
        dst[...] = val.astype(dst.dtype)


def _add_casts(casts, grid, in_specs, args, out_specs, out_shape):
    nbytes = 0
    for job in casts:
        plan = _plan_cast(job, grid)
        in_specs.extend(plan.in_specs)
        args.extend(plan.args)
        out_specs.append(plan.out_spec)
        out_shape.append(plan.out_shape)
        nbytes += plan.nbytes
    return nbytes


def _rmsnorm_kernel(x_ref, g_ref, o_ref):
    o_ref[...] = _rmsnorm_rows(x_ref[...], g_ref[...]).astype(o_ref.dtype)


def rmsnorm(x, g, out_dtype, tm=512):
    m, d = x.shape
    blocks = 2 * (_nbytes((tm, d), x.dtype) + _nbytes((tm, d), out_dtype))
    return pl.pallas_call(
        _rmsnorm_kernel,
        grid=(m // tm,),
        in_specs=[pl.BlockSpec((tm, d), lambda i: (i, 0)), pl.BlockSpec((1, d), lambda i: (0, 0))],
        out_specs=pl.BlockSpec((tm, d), lambda i: (i, 0)),
        out_shape=jax.ShapeDtypeStruct((m, d), out_dtype),
        compiler_params=pltpu.CompilerParams(
            dimension_semantics=("arbitrary",), vmem_limit_bytes=_vmem_limit(blocks)),
        name="rmsnorm",
    )(x, g.reshape(1, d))


def _accumulate_row_stats(ss_ref, tile, first):
    part = _row_sumsq(tile)

    @pl.when(first)
    def _():
        ss_ref[...] = part

    @pl.when(jnp.logical_not(first))
    def _():
        ss_ref[...] += part


def _mm_kernel(*refs, relu2, residual, scaled, stats, side, cast_gains, inv_d):
    it = iter(refs)
    x_ref, w_ref = next(it), next(it)
    res_ref = next(it) if residual else None
    ss_in_ref = next(it) if scaled else None
    side_w_ref = next(it) if side else None
    cast_in = _take_cast_refs(it, cast_gains)
    o_ref = next(it)
    ob_ref, ss_out_ref = (next(it), next(it)) if stats else (None, None)
    side_o_ref = next(it) if side else None
    cast_out = [next(it) for _ in cast_gains]

    j = pl.program_id(1)
    x = x_ref[...]
    acc = jnp.dot(x, w_ref[...], preferred_element_type=F32)
    if scaled:
        row_scale = lax.rsqrt(ss_in_ref[...] * inv_d + EPS)
        acc = acc * _lane_tile(row_scale, acc.shape[1])
    if relu2:
        acc = jnp.maximum(acc, 0.0)
        acc = acc * acc
    if residual:
        acc = res_ref[...] + acc
    o_ref[...] = acc.astype(o_ref.dtype)
    if stats:
        ob_ref[...] = acc.astype(ob_ref.dtype)
        _accumulate_row_stats(ss_out_ref, acc, j == 0)
    if side:
        @pl.when(j == 0)
        def _():
            extra = jnp.dot(x, side_w_ref[...], preferred_element_type=F32)
            side_o_ref[...] = extra * row_scale if scaled else extra
    _run_casts(cast_in, cast_out)


def matmul(x, w, out_dtype, *, n=None, relu2=False, residual=None, row_ss=None, stats=False, side_w=None,
           casts=(), tm=1024, tn=1024, name="matmul"):
    m, k = x.shape
    n = w.shape[1] if n is None else n
    tn = min(tn, n)
    assert m % tm == 0 and n % tn == 0
    grid = (m // tm, n // tn)
    row_block = pl.BlockSpec((tm, LANES), lambda i, j: (i, 0))
    tile = pl.BlockSpec((tm, tn), lambda i, j: (i, j))
    in_specs = [pl.BlockSpec((tm, k), lambda i, j: (i, 0)), pl.BlockSpec((k, tn), lambda i, j: (0, j))]
    args = [x, w]
    blocks = _nbytes((tm, k), x.dtype) + _nbytes((k, tn), w.dtype) + _nbytes((tm, tn), out_dtype)
    if residual is not None:
        in_specs.append(tile)
        args.append(residual)
        blocks += _nbytes((tm, tn), residual.dtype)
    if row_ss is not None:
        in_specs.append(row_block)
        args.append(row_ss)
    if side_w is not None:
        in_specs.append(pl.BlockSpec((k, LANES), lambda i, j: (0, 0)))
        args.append(side_w)
        blocks += _nbytes((k, LANES), side_w.dtype)
    out_specs = [tile]
    out_shape = [jax.ShapeDtypeStruct((m, n), out_dtype)]
    if stats:
        out_specs += [tile, row_block]
        out_shape += [jax.ShapeDtypeStruct((m, n), BF16), jax.ShapeDtypeStruct((m, LANES), F32)]
        blocks += _nbytes((tm, tn), BF16)
    if side_w is not None:
        out_specs.append(row_block)
        out_shape.append(jax.ShapeDtypeStruct((m, LANES), F32))
    blocks += _add_casts(casts, grid, in_specs, args, out_specs, out_shape)
    return pl.pallas_call(
        functools.partial(_mm_kernel, relu2=relu2, residual=residual is not None, scaled=row_ss is not None,
                          stats=stats, side=side_w is not None,
                          cast_gains=tuple(job.gain is not None for job in casts), inv_d=1.0 / k),
        grid=grid,
        in_specs=in_specs,
        out_specs=out_specs,
        out_shape=out_shape,
        compiler_params=pltpu.CompilerParams(
            dimension_semantics=("arbitrary", "arbitrary"),
            vmem_limit_bytes=_vmem_limit(2 * blocks + _nbytes((tm, tn), F32))),
        name=name,
    )(*args)


def _mm_ktiled_kernel(x_ref, w_ref, res_ref, o_ref, *stat_refs):
    j, kk = pl.program_id(1), pl.program_id(2)
    part = jnp.dot(x_ref[...], w_ref[...], preferred_element_type=F32)

    @pl.when(kk == 0)
    def _():
        o_ref[...] = res_ref[...] + part

    @pl.when(kk > 0)
    def _():
        o_ref[...] += part

    if stat_refs:
        ob_ref, ss_ref = stat_refs

        @pl.when(kk == pl.num_programs(2) - 1)
        def _():
            out = o_ref[...]
            ob_ref[...] = out.astype(ob_ref.dtype)
            _accumulate_row_stats(ss_ref, out, j == 0)


def matmul_ktiled(x, w, residual, *, stats=False, tm=1024, tn=1024, tk=4096, name="matmul_ktiled"):
    m, k = x.shape
    n = w.shape[1]
    assert m % tm == 0 and n % tn == 0 and k % tk == 0
    tile = pl.BlockSpec((tm, tn), lambda i, j, kk: (i, j))
    out_specs = [tile]
    out_shape = [jax.ShapeDtypeStruct((m, n), F32)]
    blocks = (_nbytes((tm, tk), x.dtype) + _nbytes((tk, tn), w.dtype) + 2 * _nbytes((tm, tn), F32))
    if stats:
        out_specs += [tile, pl.BlockSpec((tm, LANES), lambda i, j, kk: (i, 0))]
        out_shape += [jax.ShapeDtypeStruct((m, n), BF16), jax.ShapeDtypeStruct((m, LANES), F32)]
        blocks += _nbytes((tm, tn), BF16)
    return pl.pallas_call(
        _mm_ktiled_kernel,
        grid=(m // tm, n // tn, k // tk),
        in_specs=[
            pl.BlockSpec((tm, tk), lambda i, j, kk: (i, kk)),
            pl.BlockSpec((tk, tn), lambda i, j, kk: (kk, j)),
            tile,
        ],
        out_specs=out_specs,
        out_shape=out_shape,
        compiler_params=pltpu.CompilerParams(
            dimension_semantics=("arbitrary", "arbitrary", "arbitrary"),
            vmem_limit_bytes=_vmem_limit(2 * blocks + _nbytes((tm, tn), F32))),
        name=name,
    )(x, w, residual)


def _pool_kernel(*refs, tm, seq, halo, cast_gains):
    it = iter(refs)
    x_ref, xp_ref, xn_ref, gmix_ref, gmlp_ref, w_ref, scale_ref = (next(it) for _ in range(7))
    cast_in = _take_cast_refs(it, cast_gains)
    h_ref, hn_ref = next(it), next(it)
    cast_out = [next(it) for _ in cast_gains]
    buf_ref = next(it)

    i = pl.program_id(1)
    nblk = pl.num_programs(1)
    gmix = gmix_ref[...]
    x = x_ref[0]
    d_model = x.shape[-1]
    group = d_model // len(POOL_WINDOWS)

    buf_ref[halo:halo + tm, :] = _rmsnorm_rows(x, gmix)
    buf_ref[0:halo, :] = jnp.where(i > 0, _rmsnorm_rows(xp_ref[0], gmix), 0.0)
    buf_ref[halo + tm:halo + tm + halo, :] = jnp.where(i < nblk - 1, _rmsnorm_rows(xn_ref[0], gmix), 0.0)

    pos = i * tm + lax.broadcasted_iota(jnp.int32, (tm, 1), 0)
    for gi, win in enumerate(POOL_WINDOWS):
        cols = slice(gi * group, (gi + 1) * group)
        half = win // 2
        acc = buf_ref[halo - half:halo - half + tm, cols]
        for off in range(-half + 1, half):
            acc = acc + buf_ref[halo + off:halo + off + tm, cols]
        count = (jnp.minimum(pos + half, seq) - jnp.maximum(pos - half, 0)).astype(F32)
        diff = acc / count - buf_ref[halo:halo + tm, cols]
        y = jnp.dot(diff.astype(BF16), w_ref[gi], preferred_element_type=F32)
        h_ref[0, :, cols] = x[:, cols] + y * scale_ref[:, cols]

    hn_ref[0] = _rmsnorm_rows(h_ref[0], gmlp_ref[...]).astype(hn_ref.dtype)
    _run_casts(cast_in, cast_out)


def pool_layer(x, g_mix, g_mlp, pool_w, pool_scale, casts=(), tm=256):
    b, s, d = x.shape
    halo = SUBLANES
    assert max(POOL_WINDOWS) // 2 <= halo and tm % halo == 0 and s % tm == 0
    ng, gd, _ = pool_w.shape
    hb = tm // halo
    grid = (b, s // tm)
    row_block = pl.BlockSpec((1, tm, d), lambda bi, i: (bi, i, 0))
    vec = pl.BlockSpec((1, d), lambda bi, i: (0, 0))
    in_specs = [
        row_block,
        pl.BlockSpec((1, halo, d), lambda bi, i: (bi, jnp.maximum(i * hb - 1, 0), 0)),
        pl.BlockSpec((1, halo, d), lambda bi, i: (bi, jnp.minimum((i + 1) * hb, s // halo - 1), 0)),
        vec,
        vec,
        pl.BlockSpec((ng, gd, gd), lambda bi, i: (0, 0, 0), pipeline_mode=pl.Buffered(1)),
        vec,
    ]
    args = [x, x, x, g_mix.reshape(1, d), g_mlp.reshape(1, d), pool_w.astype(BF16), pool_scale.reshape(1, d)]
    out_specs = [row_block, row_block]
    out_shape = [jax.ShapeDtypeStruct((b, s, d), F32), jax.ShapeDtypeStruct((b, s, d), BF16)]
    blocks = (2 * _nbytes((tm, d), F32) + _nbytes((tm, d), BF16)) * 2 + _nbytes(pool_w.shape, BF16)
    blocks += 2 * _add_casts(casts, grid, in_specs, args, out_specs, out_shape)
    scratch_bytes = _nbytes((tm + 2 * halo, d), F32)
    return pl.pallas_call(
        functools.partial(_pool_kernel, tm=tm, seq=s, halo=halo,
                          cast_gains=tuple(job.gain is not None for job in casts)),
        grid=grid,
        in_specs=in_specs,
        out_specs=out_specs,
        out_shape=out_shape,
        scratch_shapes=[pltpu.VMEM((tm + 2 * halo, d), F32)],
        compiler_params=pltpu.CompilerParams(
            dimension_semantics=("arbitrary", "arbitrary"),
            vmem_limit_bytes=_vmem_limit(blocks + scratch_bytes)),
        name="pool_layer",
    )(*args)


def _gla_block(q_ref, k_ref, v_ref, r_ref, wup_ref, bias_ref, state_ref, *, head, dk, dv, reverse):
    rows = q_ref.shape[1]
    c = CHUNK
    ns = rows // c
    order = tuple(reversed(range(ns))) if reverse else tuple(range(ns))
    kcols = slice(head * dk, (head + 1) * dk)
    vcols = slice(head * dv, (head + 1) * dv)

    def load(ref, cols):
        return jnp.concatenate([ref[0, ci * c:(ci + 1) * c, cols] for ci in order], axis=0)

    q = load(q_ref, kcols) * (dk ** -0.5)
    k = load(k_ref, kcols)
    v = load(v_ref, vcols).astype(BF16)
    z = jnp.dot(load(r_ref, slice(None)).astype(BF16), wup_ref[0, :, kcols], preferred_element_type=F32)
    g = jax.nn.log_sigmoid(z + bias_ref[0, :, kcols]) * (math.log2(math.e) / GATE_TAU)

    row = lax.broadcasted_iota(jnp.int32, (rows, rows), 0)
    col = lax.broadcasted_iota(jnp.int32, (rows, rows), 1)
    shift = c.bit_length() - 1
    row_chunk = lax.shift_right_logical(row, shift)
    col_chunk = lax.shift_right_logical(col, shift)
    causal = (col >= row) if reverse else (col <= row)
    same_chunk = (row_chunk == col_chunk) & causal
    tri = jnp.where(same_chunk, 1.0, 0.0).astype(BF16)
    g_hi = g.astype(BF16)
    g_lo = (g - g_hi.astype(F32)).astype(BF16)
    b = jnp.dot(tri, g_hi, preferred_element_type=F32) + jnp.dot(tri, g_lo, preferred_element_type=F32)

    last_i, mid_i = (0, c - c // 2) if reverse else (c - 1, c // 2 - 1)
    b_last = [b[p * c + last_i:p * c + last_i + 1, :] for p in range(ns)]
    b_mid = [b[p * c + mid_i:p * c + mid_i + 1, :] for p in range(ns)]
    a = [jnp.exp2(x) for x in b_last]

    def prod(factors):
        out = None
        for f in factors:
            out = f if out is None else out * f
        return out

    q_intra, k_intra, q_inter, k_state, q_prev, k_next = [], [], [], [], [], []
    q_skip = {d: [] for d in range(2, ns)}
    for p in range(ns):
        rs = slice(p * c, (p + 1) * c)
        bp, qp, kp = b[rs], q[rs], k[rs]
        qn = qp * jnp.exp2(bp)
        ks = kp * jnp.exp2(b_last[p] - bp)
        q_intra.append((qp * jnp.exp2(bp - b_mid[p])).astype(BF16))
        k_intra.append((kp * jnp.exp2(b_mid[p] - bp)).astype(BF16))
        q_inter.append(qn.astype(BF16))
        k_state.append(ks.astype(BF16))
        before, after = prod(a[:p]), prod(a[p + 1:])
        q_prev.append((qn if before is None else qn * before).astype(BF16))
        k_next.append((ks if after is None else ks * after).astype(BF16))
        for d in range(2, p + 1):
            q_skip[d].append((qn * prod(a[p - d + 1:p])).astype(BF16))

    nt = (((1,), (1,)), ((), ()))
    s_intra = lax.dot_general(jnp.concatenate(q_intra, axis=0), jnp.concatenate(k_intra, axis=0), nt,
                              preferred_element_type=F32)
    cross_lhs = q_inter + [x for d in range(2, ns) for x in q_skip[d]]
    s_cross = lax.dot_general(jnp.concatenate(cross_lhs, axis=0), jnp.concatenate(k_state, axis=0), nt,
                              preferred_element_type=F32)

    skip_base = {}
    base = ns * c
    for d in range(2, ns):
        skip_base[d] = base
        base += (ns - d) * c
    score_rows = []
    row_in = lax.broadcasted_iota(jnp.int32, (c, rows), 0)
    col_in = lax.broadcasted_iota(jnp.int32, (c, rows), 1)
    col_chunk_in = lax.shift_right_logical(col_in, shift)
    for p in range(ns):
        rs = slice(p * c, (p + 1) * c)
        col_local = col_in - p * c
        causal_in = (col_local >= row_in) if reverse else (col_local <= row_in)
        sc = jnp.where((col_chunk_in == p) & causal_in, s_intra[rs], 0.0)
        if p >= 1:
            sc = jnp.where(col_chunk_in == p - 1, s_cross[rs], sc)
        for d in range(2, p + 1):
            off = skip_base[d] + (p - d) * c
            sc = jnp.where(col_chunk_in == p - d, s_cross[off:off + c], sc)
        score_rows.append(sc.astype(BF16))
    scores = jnp.concatenate(score_rows, axis=0)

    state = state_ref[head]
    o = jnp.dot(scores, v, preferred_element_type=F32)
    o = o + jnp.dot(jnp.concatenate(q_prev, axis=0), state.astype(BF16), preferred_element_type=F32)

    upd = lax.dot_general(jnp.concatenate(k_next, axis=0), v, (((0,), (0,)), ((), ())),
                          preferred_element_type=F32)
    decay_t = jnp.transpose(jnp.broadcast_to(prod(a), (LANES, dk)))
    state_ref[head] = _lane_tile(decay_t, dv) * state + upd
    return [o[p * c:(p + 1) * c] for p in range(ns)], order


def _gla_kernel(*refs, nb, heads, dk, dv, cast_gains):
    it = iter(refs)
    q_ref, k_ref, v_ref, r_ref, wup_ref, bias_ref, gate_ref, gnorm_ref = (next(it) for _ in range(8))
    cast_in = _take_cast_refs(it, cast_gains)
    o_ref = next(it)
    cast_out = [next(it) for _ in cast_gains]
    state_ref, acc_ref = next(it), next(it)

    s = pl.program_id(2)
    rows = q_ref.shape[1]
    c = CHUNK
    block = functools.partial(_gla_block, q_ref, k_ref, v_ref, r_ref, wup_ref, bias_ref, state_ref, dk=dk, dv=dv)

    @pl.when((s == 0) | (s == nb))
    def _():
        state_ref[...] = jnp.zeros_like(state_ref)

    @pl.when(s < nb)
    def _():
        base = pl.multiple_of(s * rows, rows)
        for head in range(heads):
            pieces, order = block(head=head, reverse=False)
            for piece, ci in zip(pieces, order):
                acc_ref[pl.ds(base + ci * c, c), head * dv:(head + 1) * dv] = piece

    @pl.when(s >= nb)
    def _():
        base = pl.multiple_of((2 * nb - 1 - s) * rows, rows)
        gnorm = gnorm_ref[...]
        for head in range(heads):
            pieces, order = block(head=head, reverse=True)
            vcols = slice(head * dv, (head + 1) * dv)
            for piece, ci in zip(pieces, order):
                o = _rmsnorm_rows(acc_ref[pl.ds(base + ci * c, c), vcols] + piece, gnorm)
                gate = gate_ref[0, ci * c:(ci + 1) * c, vcols]
                o_ref[0, ci * c:(ci + 1) * c, vcols] = (o * jax.nn.silu(gate)).astype(o_ref.dtype)

    _run_casts(cast_in, cast_out)


def gla_core(p, r, wup, bias, g_norm, *, heads, casts=(), bs=256, heads_per_step=2):
    b, s, width = p.shape
    dv_total = width // 3
    dk_total = dv_total // 2
    dk, dv = dk_total // heads, dv_total // heads
    hp = heads_per_step
    assert heads % hp == 0 and s % bs == 0
    gk, gv = hp * dk, hp * dv
    nb = s // bs
    kq = dk_total // gk
    kv = (2 * dk_total) // gv
    kg = (2 * dk_total + dv_total) // gv
    rw = r.shape[-1]
    grid = (b, heads // hp, 2 * nb)

    def blk(i):
        return jnp.where(i < nb, i, 2 * nb - 1 - i)

    def out_blk(i):
        return jnp.where(i < nb, nb - 1, 2 * nb - 1 - i)

    in_specs = [
        pl.BlockSpec((1, bs, gk), lambda bi, h, i: (bi, blk(i), h)),
        pl.BlockSpec((1, bs, gk), lambda bi, h, i: (bi, blk(i), kq + h)),
        pl.BlockSpec((1, bs, gv), lambda bi, h, i: (bi, blk(i), kv + h)),
        pl.BlockSpec((1, bs, rw), lambda bi, h, i: (bi, blk(i), 0)),
        pl.BlockSpec((1, rw, gk), lambda bi, h, i: (i // nb, 0, h)),
        pl.BlockSpec((1, 1, gk), lambda bi, h, i: (i // nb, 0, h)),
        pl.BlockSpec((1, bs, gv), lambda bi, h, i: (bi, out_blk(i), kg + h)),
        pl.BlockSpec((1, dv), lambda bi, h, i: (0, 0)),
    ]
    args = [p, p, p, r, wup, bias, p, g_norm.reshape(1, dv)]
    out_specs = [pl.BlockSpec((1, bs, gv), lambda bi, h, i: (bi, out_blk(i), h))]
    out_shape = [jax.ShapeDtypeStruct((b, s, dv_total), BF16)]
    blocks = (2 * _nbytes((bs, gk), F32) + 2 * _nbytes((bs, gv), F32) + _nbytes((bs, rw), F32)
              + _nbytes((bs, gv), BF16))
    blocks += _add_casts(casts, grid, in_specs, args, out_specs, out_shape)
    scratch_bytes = _nbytes((hp, dk, dv), F32) + _nbytes((s, gv), F32)
    return pl.pallas_call(
        functools.partial(_gla_kernel, nb=nb, heads=hp, dk=dk, dv=dv,
                          cast_gains=tuple(job.gain is not None for job in casts)),
        grid=grid,
        in_specs=in_specs,
        out_specs=out_specs,
        out_shape=out_shape,
        scratch_shapes=[pltpu.VMEM((hp, dk, dv), F32), pltpu.VMEM((s, gv), F32)],
        compiler_params=pltpu.CompilerParams(
            dimension_semantics=("arbitrary", "arbitrary", "arbitrary"),
            vmem_limit_bytes=_vmem_limit(2 * blocks + scratch_bytes)),
        name="gla_scan",
    )(*args)


def gla_layer(h, g_mix, w_in, w_out_f32, layer, w_up_f, b_up_f, w_up_b, b_up_b, g_norm, next_cast):
    b, s, d = h.shape
    t = b * s
    main = w_in.shape[1] - 2 * GATE_RANK
    h2 = h.reshape(t, d)
    hn = rmsnorm(h2, g_mix, BF16)
    w_r = jnp.pad(w_in[:, main:], ((0, 0), (0, LANES - 2 * GATE_RANK)))
    p, r, w_out = matmul(hn, w_in, F32, n=main, side_w=w_r, casts=(CastJob(w_out_f32, layer, period=3),),
                         name="gla_in_proj")
    wup = jnp.stack([jnp.pad(w_up_f, ((0, LANES - GATE_RANK), (0, 0))),
                     jnp.pad(w_up_b, ((GATE_RANK, LANES - 2 * GATE_RANK), (0, 0)))]).astype(BF16)
    bias = jnp.stack([b_up_f, b_up_b])[:, None, :]
    og, next_bf16 = gla_core(p.reshape(b, s, main), r.reshape(b, s, LANES), wup, bias, g_norm, heads=GLA_HEADS,
                             casts=(next_cast,))
    out, out_b, out_ss = matmul(og.reshape(t, -1), w_out, F32, residual=h2, stats=True, tn=512,
                                name="gla_out_proj")
    return out, out_b, out_ss, next_bf16


def kernel(x, norm_mix, norm_mlp, norm_final, pool_w, pool_scale, gla_w_in, gla_w_up_f, gla_b_up_f, gla_w_up_b,
           gla_b_up_b, gla_g_norm, gla_w_out, mlp_w_in, mlp_w_out):
    b, s, d = x.shape
    t = b * s
    h, hn, w1 = pool_layer(x, norm_mix[0], norm_mlp[0], pool_w[0], pool_scale[0], casts=(CastJob(mlp_w_in, 0),))
    a, w2, w_in = matmul(hn.reshape(t, d), w1, BF16, relu2=True,
                         casts=(CastJob(mlp_w_out, 0), CastJob(gla_w_in, 0)), name="mlp_up_proj")
    (h,) = matmul_ktiled(a, w2, h.reshape(t, d), name="mlp_down_proj")

    h, hb, h_ss, w1 = gla_layer(h.reshape(b, s, d), norm_mix[1], w_in, gla_w_out, 0, gla_w_up_f[0], gla_b_up_f[0],
                                gla_w_up_b[0], gla_b_up_b[0], gla_g_norm[0],
                                CastJob(mlp_w_in, 1, gain=norm_mlp[1]))

    a, w2 = matmul(hb, w1, BF16, relu2=True, row_ss=h_ss, casts=(CastJob(mlp_w_out, 1),), name="mlp_up_proj")
    (h,) = matmul_ktiled(a, w2, h, name="mlp_down_proj")
    return rmsnorm(h, norm_final, F32).reshape(b, s, d)
```

```python
import functools
import math
from typing import NamedTuple, Optional

import jax
import jax.numpy as jnp
from jax import lax
from jax.experimental import pallas as pl
from jax.experimental.pallas import tpu as pltpu

EPS = 1e-6
POOL_WINDOWS = (2, 4, 8, 16)
GLA_HEADS = 4
GATE_RANK = 16
GATE_TAU = 16.0
CHUNK = 64

V7X_VMEM_BYTES = 64 * 1024 * 1024
LANES = 128
SUBLANES = 8

F32 = jnp.float32
BF16 = jnp.bfloat16


def _vmem_limit(block_bytes):
    return int(min(V7X_VMEM_BYTES - 4 * 1024 * 1024, 2 * block_bytes))


def _nbytes(shape, dtype):
    n = 1
    for s in shape:
        n *= s
    return n * jnp.dtype(dtype).itemsize


def _lane_tile(x, n):
    return jnp.concatenate([x] * (n // LANES), axis=1)


def _row_sumsq(x):
    return jnp.broadcast_to(jnp.sum(x * x, axis=-1, keepdims=True), (x.shape[0], LANES))


def _rmsnorm_rows(x, g):
    y = x * lax.rsqrt(jnp.mean(x * x, axis=-1, keepdims=True) + EPS)
    return y * g


class CastJob(NamedTuple):
    src: jax.Array
    layer: int
    gain: Optional[jax.Array] = None


class _CastPlan(NamedTuple):
    in_specs: list
    args: list
    out_spec: pl.BlockSpec
    out_shape: jax.ShapeDtypeStruct
    nbytes: int


def _plan_cast(job, grid):
    _, r, c = job.src.shape
    steps = 1
    for g in grid:
        steps *= g
    bf16_rows = 2 * SUBLANES
    nslabs = next(n for n in range(min(steps, r // bf16_rows), 0, -1) if r % (n * bf16_rows) == 0)
    slab = r // nslabs

    def slab_of(*idx):
        step = 0
        for g, i in zip(grid, idx):
            step = step * g + i
        return step * nslabs // steps

    layer = job.layer
    in_specs = [pl.BlockSpec((None, slab, c), lambda *idx: (layer, slab_of(*idx), 0))]
    args = [job.src]
    if job.gain is not None:
        in_specs.append(pl.BlockSpec((slab, 1), lambda *idx: (slab_of(*idx), 0)))
        args.append(job.gain.reshape(r, 1))
    return _CastPlan(in_specs, args, pl.BlockSpec((slab, c), lambda *idx: (slab_of(*idx), 0)),
                     jax.ShapeDtypeStruct((r, c), BF16), _nbytes((slab, c), F32) + _nbytes((slab, c), BF16))


def _take_cast_refs(it, gains):
    return [(next(it), next(it) if g else None) for g in gains]


def _run_casts(cast_in, cast_out):
    for (src, gain), dst in zip(cast_in, cast_out):
        val = src[...] if gain is None else src[...] * gain[...]
        dst[...] = val.astype(dst.dtype)


def _add_casts(casts, grid, in_specs, args, out_specs, out_shape):
    nbytes = 0
    for job in casts:
        plan = _plan_cast(job, grid)
        in_specs.extend(plan.in_specs)
        args.extend(plan.args)
        out_specs.append(plan.out_spec)
        out_shape.append(plan.out_shape)
        nbytes += plan.nbytes
    return nbytes


def _rmsnorm_kernel(x_ref, g_ref, o_ref):
    o_ref[...] = _rmsnorm_rows(x_ref[...], g_ref[...]).astype(o_ref.dtype)


def rmsnorm(x, g, out_dtype, tm=512):
    m, d = x.shape
    blocks = 2 * (_nbytes((tm, d), x.dtype) + _nbytes((tm, d), out_dtype))
    return pl.pallas_call(
        _rmsnorm_kernel,
        grid=(m // tm,),
        in_specs=[pl.BlockSpec((tm, d), lambda i: (i, 0)), pl.BlockSpec((1, d), lambda i: (0, 0))],
        out_specs=pl.BlockSpec((tm, d), lambda i: (i, 0)),
        out_shape=jax.ShapeDtypeStruct((m, d), out_dtype),
        compiler_params=pltpu.CompilerParams(
            dimension_semantics=("arbitrary",), vmem_limit_bytes=_vmem_limit(blocks)),
        name="rmsnorm",
    )(x, g.reshape(1, d))


def _accumulate_row_stats(ss_ref, tile, first):
    part = _row_sumsq(tile)

    @pl.when(first)
    def _():
        ss_ref[...] = part

    @pl.when(jnp.logical_not(first))
    def _():
        ss_ref[...] += part


def _mm_kernel(*refs, relu2, residual, scaled, stats, side, cast_gains, inv_d, w_dims):
    it = iter(refs)
    x_ref, w_ref = next(it), next(it)
    res_ref = next(it) if residual else None
    ss_in_ref = next(it) if scaled else None
    side_w_ref = next(it) if side else None
    cast_in = _take_cast_refs(it, cast_gains)
    o_ref = next(it)
    ob_ref, ss_out_ref = (next(it), next(it)) if stats else (None, None)
    side_o_ref = next(it) if side else None
    cast_out = [next(it) for _ in cast_gains]

    j = pl.program_id(1)
    x = x_ref[...]
    acc = lax.dot_general(x, w_ref[...], w_dims, preferred_element_type=F32)
    if scaled:
        row_scale = lax.rsqrt(ss_in_ref[...] * inv_d + EPS)
        acc = acc * _lane_tile(row_scale, acc.shape[1])
    if relu2:
        acc = jnp.maximum(acc, 0.0)
        acc = acc * acc
    if residual:
        acc = res_ref[...] + acc
    o_ref[...] = acc.astype(o_ref.dtype)
    if stats:
        ob_ref[...] = acc.astype(ob_ref.dtype)
        _accumulate_row_stats(ss_out_ref, acc, j == 0)
    if side:
        @pl.when(j == 0)
        def _():
            extra = lax.dot_general(x, side_w_ref[...], w_dims, preferred_element_type=F32)
            side_o_ref[...] = extra * row_scale if scaled else extra
    _run_casts(cast_in, cast_out)


def matmul(x, w, out_dtype, *, n=None, w_transposed=False, relu2=False, residual=None, row_ss=None, stats=False,
           side_w=None, casts=(), tm=1024, tn=1024, name="matmul"):
    m, k = x.shape
    n = w.shape[0 if w_transposed else 1] if n is None else n
    tn = min(tn, n)
    assert m % tm == 0 and n % tn == 0
    grid = (m // tm, n // tn)
    row_block = pl.BlockSpec((tm, LANES), lambda i, j: (i, 0))
    tile = pl.BlockSpec((tm, tn), lambda i, j: (i, j))
    if w_transposed:
        w_spec, side_spec = pl.BlockSpec((tn, k), lambda i, j: (j, 0)), pl.BlockSpec((LANES, k), lambda i, j: (0, 0))
    else:
        w_spec, side_spec = pl.BlockSpec((k, tn), lambda i, j: (0, j)), pl.BlockSpec((k, LANES), lambda i, j: (0, 0))
    in_specs = [pl.BlockSpec((tm, k), lambda i, j: (i, 0)), w_spec]
    args = [x, w]
    blocks = _nbytes((tm, k), x.dtype) + _nbytes((k, tn), w.dtype) + _nbytes((tm, tn), out_dtype)
    if residual is not None:
        in_specs.append(tile)
        args.append(residual)
        blocks += _nbytes((tm, tn), residual.dtype)
    if row_ss is not None:
        in_specs.append(row_block)
        args.append(row_ss)
    if side_w is not None:
        in_specs.append(side_spec)
        args.append(side_w)
        blocks += _nbytes((k, LANES), side_w.dtype)
    out_specs = [tile]
    out_shape = [jax.ShapeDtypeStruct((m, n), out_dtype)]
    if stats:
        out_specs += [tile, row_block]
        out_shape += [jax.ShapeDtypeStruct((m, n), BF16), jax.ShapeDtypeStruct((m, LANES), F32)]
        blocks += _nbytes((tm, tn), BF16)
    if side_w is not None:
        out_specs.append(row_block)
        out_shape.append(jax.ShapeDtypeStruct((m, LANES), F32))
    blocks += _add_casts(casts, grid, in_specs, args, out_specs, out_shape)
    return pl.pallas_call(
        functools.partial(_mm_kernel, relu2=relu2, residual=residual is not None, scaled=row_ss is not None,
                          stats=stats, side=side_w is not None,
                          cast_gains=tuple(job.gain is not None for job in casts), inv_d=1.0 / k,
                          w_dims=(((1,), (1 if w_transposed else 0,)), ((), ()))),
        grid=grid,
        in_specs=in_specs,
        out_specs=out_specs,
        out_shape=out_shape,
        compiler_params=pltpu.CompilerParams(
            dimension_semantics=("arbitrary", "arbitrary"),
            vmem_limit_bytes=_vmem_limit(2 * blocks + _nbytes((tm, tn), F32))),
        name=name,
    )(*args)


def _mm_ktiled_kernel(x_ref, w_ref, res_ref, o_ref, *stat_refs):
    j, kk = pl.program_id(1), pl.program_id(2)
    part = jnp.dot(x_ref[...], w_ref[...], preferred_element_type=F32)

    @pl.when(kk == 0)
    def _():
        o_ref[...] = res_ref[...] + part

    @pl.when(kk > 0)
    def _():
        o_ref[...] += part

    if stat_refs:
        ob_ref, ss_ref = stat_refs

        @pl.when(kk == pl.num_programs(2) - 1)
        def _():
            out = o_ref[...]
            ob_ref[...] = out.astype(ob_ref.dtype)
            _accumulate_row_stats(ss_ref, out, j == 0)


def matmul_ktiled(x, w, residual, *, stats=False, tm=1024, tn=1024, tk=4096, name="matmul_ktiled"):
    m, k = x.shape
    n = w.shape[1]
    assert m % tm == 0 and n % tn == 0 and k % tk == 0
    tile = pl.BlockSpec((tm, tn), lambda i, j, kk: (i, j))
    out_specs = [tile]
    out_shape = [jax.ShapeDtypeStruct((m, n), F32)]
    blocks = (_nbytes((tm, tk), x.dtype) + _nbytes((tk, tn), w.dtype) + 2 * _nbytes((tm, tn), F32))
    if stats:
        out_specs += [tile, pl.BlockSpec((tm, LANES), lambda i, j, kk: (i, 0))]
        out_shape += [jax.ShapeDtypeStruct((m, n), BF16), jax.ShapeDtypeStruct((m, LANES), F32)]
        blocks += _nbytes((tm, tn), BF16)
    return pl.pallas_call(
        _mm_ktiled_kernel,
        grid=(m // tm, n // tn, k // tk),
        in_specs=[
            pl.BlockSpec((tm, tk), lambda i, j, kk: (i, kk)),
            pl.BlockSpec((tk, tn), lambda i, j, kk: (kk, j)),
            tile,
        ],
        out_specs=out_specs,
        out_shape=out_shape,
        compiler_params=pltpu.CompilerParams(
            dimension_semantics=("arbitrary", "arbitrary", "arbitrary"),
            vmem_limit_bytes=_vmem_limit(2 * blocks + _nbytes((tm, tn), F32))),
        name=name,
    )(x, w, residual)


def _pool_kernel(*refs, tm, seq, halo, cast_gains):
    it = iter(refs)
    x_ref, xp_ref, xn_ref, gmix_ref, gmlp_ref, w_ref, scale_ref = (next(it) for _ in range(7))
    cast_in = _take_cast_refs(it, cast_gains)
    h_ref, hn_ref = next(it), next(it)
    cast_out = [next(it) for _ in cast_gains]
    buf_ref = next(it)

    i = pl.program_id(1)
    nblk = pl.num_programs(1)
    gmix = gmix_ref[...]
    x = x_ref[0]
    d_model = x.shape[-1]
    group = d_model // len(POOL_WINDOWS)

    buf_ref[halo:halo + tm, :] = _rmsnorm_rows(x, gmix)
    buf_ref[0:halo, :] = jnp.where(i > 0, _rmsnorm_rows(xp_ref[0], gmix), 0.0)
    buf_ref[halo + tm:halo + tm + halo, :] = jnp.where(i < nblk - 1, _rmsnorm_rows(xn_ref[0], gmix), 0.0)

    pos = i * tm + lax.broadcasted_iota(jnp.int32, (tm, 1), 0)
    for gi, win in enumerate(POOL_WINDOWS):
        cols = slice(gi * group, (gi + 1) * group)
        half = win // 2
        acc = buf_ref[halo - half:halo - half + tm, cols]
        for off in range(-half + 1, half):
            acc = acc + buf_ref[halo + off:halo + off + tm, cols]
        count = (jnp.minimum(pos + half, seq) - jnp.maximum(pos - half, 0)).astype(F32)
        diff = acc * (1.0 / count) - buf_ref[halo:halo + tm, cols]
        y = jnp.dot(diff.astype(BF16), w_ref[gi], preferred_element_type=F32)
        h_ref[0, :, cols] = x[:, cols] + y * scale_ref[:, cols]

    hn_ref[0] = _rmsnorm_rows(h_ref[0], gmlp_ref[...]).astype(hn_ref.dtype)
    _run_casts(cast_in, cast_out)


def pool_layer(x, g_mix, g_mlp, pool_w, pool_scale, casts=(), tm=256):
    b, s, d = x.shape
    halo = SUBLANES
    assert max(POOL_WINDOWS) // 2 <= halo and tm % halo == 0 and s % tm == 0
    ng, gd, _ = pool_w.shape
    hb = tm // halo
    grid = (b, s // tm)
    row_block = pl.BlockSpec((1, tm, d), lambda bi, i: (bi, i, 0))
    vec = pl.BlockSpec((1, d), lambda bi, i: (0, 0))
    in_specs = [
        row_block,
        pl.BlockSpec((1, halo, d), lambda bi, i: (bi, jnp.maximum(i * hb - 1, 0), 0)),
        pl.BlockSpec((1, halo, d), lambda bi, i: (bi, jnp.minimum((i + 1) * hb, s // halo - 1), 0)),
        vec,
        vec,
        pl.BlockSpec((ng, gd, gd), lambda bi, i: (0, 0, 0), pipeline_mode=pl.Buffered(1)),
        vec,
    ]
    args = [x, x, x, g_mix.reshape(1, d), g_mlp.reshape(1, d), pool_w.astype(BF16), pool_scale.reshape(1, d)]
    out_specs = [row_block, row_block]
    out_shape = [jax.ShapeDtypeStruct((b, s, d), F32), jax.ShapeDtypeStruct((b, s, d), BF16)]
    blocks = (2 * _nbytes((tm, d), F32) + _nbytes((tm, d), BF16)) * 2 + _nbytes(pool_w.shape, BF16)
    blocks += 2 * _add_casts(casts, grid, in_specs, args, out_specs, out_shape)
    scratch_bytes = _nbytes((tm + 2 * halo, d), F32)
    return pl.pallas_call(
        functools.partial(_pool_kernel, tm=tm, seq=s, halo=halo,
                          cast_gains=tuple(job.gain is not None for job in casts)),
        grid=grid,
        in_specs=in_specs,
        out_specs=out_specs,
        out_shape=out_shape,
        scratch_shapes=[pltpu.VMEM((tm + 2 * halo, d), F32)],
        compiler_params=pltpu.CompilerParams(
            dimension_semantics=("arbitrary", "arbitrary"),
            vmem_limit_bytes=_vmem_limit(blocks + scratch_bytes)),
        name="pool_layer",
    )(*args)


def _gla_block(q_ref, k_ref, v_ref, r_ref, wup_ref, bias_ref, state_ref, *, head, dk, dv, reverse):
    rows = q_ref.shape[1]
    c = CHUNK
    ns = rows // c
    order = tuple(reversed(range(ns))) if reverse else tuple(range(ns))
    kcols = slice(head * dk, (head + 1) * dk)
    vcols = slice(head * dv, (head + 1) * dv)

    def load(ref, cols):
        return jnp.concatenate([ref[0, ci * c:(ci + 1) * c, cols] for ci in order], axis=0)

    q = load(q_ref, kcols) * (dk ** -0.5)
    k = load(k_ref, kcols)
    v = load(v_ref, vcols).astype(BF16)
    z = jnp.dot(load(r_ref, slice(None)).astype(BF16), wup_ref[0, :, kcols], preferred_element_type=F32)
    g = jax.nn.log_sigmoid(z + bias_ref[0, :, kcols]) * (math.log2(math.e) / GATE_TAU)

    row = lax.broadcasted_iota(jnp.int32, (rows, rows), 0)
    col = lax.broadcasted_iota(jnp.int32, (rows, rows), 1)
    shift = c.bit_length() - 1
    row_chunk = lax.shift_right_logical(row, shift)
    col_chunk = lax.shift_right_logical(col, shift)
    causal = (col >= row) if reverse else (col <= row)
    same_chunk = (row_chunk == col_chunk) & causal
    tri = jnp.where(same_chunk, 1.0, 0.0).astype(BF16)
    g_hi = g.astype(BF16)
    g_lo = (g - g_hi.astype(F32)).astype(BF16)
    b = jnp.dot(tri, g_hi, preferred_element_type=F32) + jnp.dot(tri, g_lo, preferred_element_type=F32)

    last_i, mid_i = (0, c - c // 2) if reverse else (c - 1, c // 2 - 1)
    b_last = [b[p * c + last_i:p * c + last_i + 1, :] for p in range(ns)]
    b_mid = [b[p * c + mid_i:p * c + mid_i + 1, :] for p in range(ns)]
    a = [jnp.exp2(x) for x in b_last]

    def prod(factors):
        out = None
        for f in factors:
            out = f if out is None else out * f
        return out

    q_intra, k_intra, q_inter, k_state, q_prev, k_next = [], [], [], [], [], []
    q_skip = {d: [] for d in range(2, ns)}
    for p in range(ns):
        rs = slice(p * c, (p + 1) * c)
        bp, qp, kp = b[rs], q[rs], k[rs]
        qn = qp * jnp.exp2(bp)
        ks = kp * jnp.exp2(b_last[p] - bp)
        q_intra.append((qp * jnp.exp2(bp - b_mid[p])).astype(BF16))
        k_intra.append((kp * jnp.exp2(b_mid[p] - bp)).astype(BF16))
        q_inter.append(qn.astype(BF16))
        k_state.append(ks.astype(BF16))
        before, after = prod(a[:p]), prod(a[p + 1:])
        q_prev.append((qn if before is None else qn * before).astype(BF16))
        k_next.append((ks if after is None else ks * after).astype(BF16))
        for d in range(2, p + 1):
            q_skip[d].append((qn * prod(a[p - d + 1:p])).astype(BF16))

    nt = (((1,), (1,)), ((), ()))
    s_intra = lax.dot_general(jnp.concatenate(q_intra, axis=0), jnp.concatenate(k_intra, axis=0), nt,
                              preferred_element_type=F32)
    cross_lhs = q_inter + [x for d in range(2, ns) for x in q_skip[d]]
    s_cross = lax.dot_general(jnp.concatenate(cross_lhs, axis=0), jnp.concatenate(k_state, axis=0), nt,
                              preferred_element_type=F32)

    skip_base = {}
    base = ns * c
    for d in range(2, ns):
        skip_base[d] = base
        base += (ns - d) * c
    score_rows = []
    row_in = lax.broadcasted_iota(jnp.int32, (c, rows), 0)
    col_in = lax.broadcasted_iota(jnp.int32, (c, rows), 1)
    col_chunk_in = lax.shift_right_logical(col_in, shift)
    for p in range(ns):
        rs = slice(p * c, (p + 1) * c)
        col_local = col_in - p * c
        causal_in = (col_local >= row_in) if reverse else (col_local <= row_in)
        sc = jnp.where((col_chunk_in == p) & causal_in, s_intra[rs], 0.0)
        if p >= 1:
            sc = jnp.where(col_chunk_in == p - 1, s_cross[rs], sc)
        for d in range(2, p + 1):
            off = skip_base[d] + (p - d) * c
            sc = jnp.where(col_chunk_in == p - d, s_cross[off:off + c], sc)
        score_rows.append(sc.astype(BF16))
    scores = jnp.concatenate(score_rows, axis=0)

    state = state_ref[head]
    o = jnp.dot(scores, v, preferred_element_type=F32)
    o = o + jnp.dot(jnp.concatenate(q_prev, axis=0), state.astype(BF16), preferred_element_type=F32)

    upd = lax.dot_general(jnp.concatenate(k_next, axis=0), v, (((0,), (0,)), ((), ())),
                          preferred_element_type=F32)
    decay_t = jnp.transpose(jnp.broadcast_to(prod(a), (LANES, dk)))
    state_ref[head] = _lane_tile(decay_t, dv) * state + upd
    return [o[p * c:(p + 1) * c] for p in range(ns)], order


def _gla_kernel(*refs, nb, heads, dk, dv, cast_gains):
    it = iter(refs)
    q_ref, k_ref, v_ref, r_ref, wup_ref, bias_ref, gate_ref, gnorm_ref = (next(it) for _ in range(8))
    cast_in = _take_cast_refs(it, cast_gains)
    o_ref = next(it)
    cast_out = [next(it) for _ in cast_gains]
    state_ref, acc_ref = next(it), next(it)

    s = pl.program_id(2)
    rows = q_ref.shape[1]
    c = CHUNK
    block = functools.partial(_gla_block, q_ref, k_ref, v_ref, r_ref, wup_ref, bias_ref, state_ref, dk=dk, dv=dv)

    @pl.when((s == 0) | (s == nb))
    def _():
        state_ref[...] = jnp.zeros_like(state_ref)

    @pl.when(s < nb)
    def _():
        base = pl.multiple_of(s * rows, rows)
        for head in range(heads):
            pieces, order = block(head=head, reverse=False)
            for piece, ci in zip(pieces, order):
                acc_ref[pl.ds(base + ci * c, c), head * dv:(head + 1) * dv] = piece

    @pl.when(s >= nb)
    def _():
        base = pl.multiple_of((2 * nb - 1 - s) * rows, rows)
        gnorm = gnorm_ref[...]
        for head in range(heads):
            pieces, order = block(head=head, reverse=True)
            vcols = slice(head * dv, (head + 1) * dv)
            for piece, ci in zip(pieces, order):
                o = _rmsnorm_rows(acc_ref[pl.ds(base + ci * c, c), vcols] + piece, gnorm)
                gate = gate_ref[0, ci * c:(ci + 1) * c, vcols]
                o_ref[0, ci * c:(ci + 1) * c, vcols] = (o * jax.nn.silu(gate)).astype(o_ref.dtype)

    _run_casts(cast_in, cast_out)


def gla_core(p, r, wup, bias, g_norm, *, heads, casts=(), bs=256, heads_per_step=2):
    b, s, width = p.shape
    dv_total = width // 3
    dk_total = dv_total // 2
    dk, dv = dk_total // heads, dv_total // heads
    hp = heads_per_step
    assert heads % hp == 0 and s % bs == 0
    gk, gv = hp * dk, hp * dv
    nb = s // bs
    kq = dk_total // gk
    kv = (2 * dk_total) // gv
    kg = (2 * dk_total + dv_total) // gv
    rw = r.shape[-1]
    grid = (b, heads // hp, 2 * nb)

    def blk(i):
        return jnp.where(i < nb, i, 2 * nb - 1 - i)

    def out_blk(i):
        return jnp.where(i < nb, nb - 1, 2 * nb - 1 - i)

    in_specs = [
        pl.BlockSpec((1, bs, gk), lambda bi, h, i: (bi, blk(i), h)),
        pl.BlockSpec((1, bs, gk), lambda bi, h, i: (bi, blk(i), kq + h)),
        pl.BlockSpec((1, bs, gv), lambda bi, h, i: (bi, blk(i), kv + h)),
        pl.BlockSpec((1, bs, rw), lambda bi, h, i: (bi, blk(i), 0)),
        pl.BlockSpec((1, rw, gk), lambda bi, h, i: (i // nb, 0, h)),
        pl.BlockSpec((1, 1, gk), lambda bi, h, i: (i // nb, 0, h)),
        pl.BlockSpec((1, bs, gv), lambda bi, h, i: (bi, out_blk(i), kg + h)),
        pl.BlockSpec((1, dv), lambda bi, h, i: (0, 0)),
    ]
    args = [p, p, p, r, wup, bias, p, g_norm.reshape(1, dv)]
    out_specs = [pl.BlockSpec((1, bs, gv), lambda bi, h, i: (bi, out_blk(i), h))]
    out_shape = [jax.ShapeDtypeStruct((b, s, dv_total), BF16)]
    blocks = (2 * _nbytes((bs, gk), F32) + 2 * _nbytes((bs, gv), F32) + _nbytes((bs, rw), F32)
              + _nbytes((bs, gv), BF16))
    blocks += _add_casts(casts, grid, in_specs, args, out_specs, out_shape)
    scratch_bytes = _nbytes((hp, dk, dv), F32) + _nbytes((s, gv), F32)
    return pl.pallas_call(
        functools.partial(_gla_kernel, nb=nb, heads=hp, dk=dk, dv=dv,
                          cast_gains=tuple(job.gain is not None for job in casts)),
        grid=grid,
        in_specs=in_specs,
        out_specs=out_specs,
        out_shape=out_shape,
        scratch_shapes=[pltpu.VMEM((hp, dk, dv), F32), pltpu.VMEM((s, gv), F32)],
        compiler_params=pltpu.CompilerParams(
            dimension_semantics=("arbitrary", "arbitrary", "arbitrary"),
            vmem_limit_bytes=_vmem_limit(2 * blocks + scratch_bytes)),
        name="gla_scan",
    )(*args)


def gla_layer(h, g_mix, w_in_t, w_out_f32, layer, w_up_f, b_up_f, w_up_b, b_up_b, g_norm, next_cast):
    b, s, d = h.shape
    t = b * s
    main = w_in_t.shape[0] - 2 * GATE_RANK
    h2 = h.reshape(t, d)
    hn = rmsnorm(h2, g_mix, BF16)
    w_r_t = jnp.pad(w_in_t[main:], ((0, LANES - 2 * GATE_RANK), (0, 0)))
    p, r, w_out = matmul(hn, w_in_t, F32, n=main, w_transposed=True, side_w=w_r_t,
                         casts=(CastJob(w_out_f32, layer),), name="gla_in_proj")
    wup = jnp.stack([jnp.pad(w_up_f, ((0, LANES - GATE_RANK), (0, 0))),
                     jnp.pad(w_up_b, ((GATE_RANK, LANES - 2 * GATE_RANK), (0, 0)))]).astype(BF16)
    bias = jnp.stack([b_up_f, b_up_b])[:, None, :]
    og, next_bf16 = gla_core(p.reshape(b, s, main), r.reshape(b, s, LANES), wup, bias, g_norm, heads=GLA_HEADS,
                             casts=(next_cast,))
    out, out_b, out_ss = matmul(og.reshape(t, -1), w_out, F32, residual=h2, stats=True, tn=512,
                                name="gla_out_proj")
    return out, out_b, out_ss, next_bf16


def kernel(x, norm_mix, norm_mlp, norm_final, pool_w, pool_scale, gla_w_in, gla_w_up_f, gla_b_up_f, gla_w_up_b,
           gla_b_up_b, gla_g_norm, gla_w_out, mlp_w_in, mlp_w_out):
    b, s, d = x.shape
    t = b * s
    w_in_t_f32 = jnp.swapaxes(gla_w_in, 1, 2)
    h, hn, w1 = pool_layer(x, norm_mix[0], norm_mlp[0], pool_w[0], pool_scale[0], casts=(CastJob(mlp_w_in, 0),))
    a, w2, w_in = matmul(hn.reshape(t, d), w1, BF16, relu2=True,
                         casts=(CastJob(mlp_w_out, 0), CastJob(w_in_t_f32, 0)),
                         name="mlp_up_proj")
    (h,) = matmul_ktiled(a, w2, h.reshape(t, d), name="mlp_down_proj")

    h, hb, h_ss, w1 = gla_layer(h.reshape(b, s, d), norm_mix[1], w_in, gla_w_out, 0, gla_w_up_f[0], gla_b_up_f[0],
                                gla_w_up_b[0], gla_b_up_b[0], gla_g_norm[0],
                                CastJob(mlp_w_in, 1, gain=norm_mlp[1]))

    a, w2 = matmul(hb, w1, BF16, relu2=True, row_ss=h_ss, casts=(CastJob(mlp_w_out, 1),), name="mlp_up_proj")
    (h,) = matmul_ktiled(a, w2, h, name="mlp_down_proj")
    return rmsnorm(h, norm_final, F32).reshape(b, s, d)
```

```python
import functools
import math
from typing import NamedTuple, Optional

import jax
import jax.numpy as jnp
from jax import lax
from jax.experimental import pallas as pl
from jax.experimental.pallas import tpu as pltpu

EPS = 1e-6
POOL_WINDOWS = (2, 4, 8, 16)
GLA_HEADS = 4
GATE_RANK = 16
GATE_TAU = 16.0
CHUNK = 64

V7X_VMEM_BYTES = 64 * 1024 * 1024
LANES = 128
SUBLANES = 8

F32 = jnp.float32
BF16 = jnp.bfloat16


def _vmem_limit(block_bytes):
    return int(min(V7X_VMEM_BYTES - 4 * 1024 * 1024, 2 * block_bytes))


def _nbytes(shape, dtype):
    n = 1
    for s in shape:
        n *= s
    return n * jnp.dtype(dtype).itemsize


def _lane_tile(x, n):
    return jnp.concatenate([x] * (n // LANES), axis=1)


def _row_sumsq(x):
    return jnp.broadcast_to(jnp.sum(x * x, axis=-1, keepdims=True), (x.shape[0], LANES))


def _rmsnorm_rows(x, g):
    y = x * lax.rsqrt(jnp.mean(x * x, axis=-1, keepdims=True) + EPS)
    return y * g


class CastJob(NamedTuple):
    src: jax.Array
    layer: int
    gain: Optional[jax.Array] = None


class _CastPlan(NamedTuple):
    in_specs: list
    args: list
    out_spec: pl.BlockSpec
    out_shape: jax.ShapeDtypeStruct
    nbytes: int


def _plan_cast(job, grid):
    _, r, c = job.src.shape
    steps = 1
    for g in grid:
        steps *= g
    bf16_rows = 2 * SUBLANES
    nslabs = next(n for n in range(min(steps, r // bf16_rows), 0, -1) if r % (n * bf16_rows) == 0)
    slab = r // nslabs

    def slab_of(*idx):
        step = 0
        for g, i in zip(grid, idx):
            step = step * g + i
        return step * nslabs // steps

    layer = job.layer
    in_specs = [pl.BlockSpec((None, slab, c), lambda *idx: (layer, slab_of(*idx), 0))]
    args = [job.src]
    if job.gain is not None:
        in_specs.append(pl.BlockSpec((slab, 1), lambda *idx: (slab_of(*idx), 0)))
        args.append(job.gain.reshape(r, 1))
    return _CastPlan(in_specs, args, pl.BlockSpec((slab, c), lambda *idx: (slab_of(*idx), 0)),
                     jax.ShapeDtypeStruct((r, c), BF16), _nbytes((slab, c), F32) + _nbytes((slab, c), BF16))


def _take_cast_refs(it, gains):
    return [(next(it), next(it) if g else None) for g in gains]


def _run_casts(cast_in, cast_out):
    for (src, gain), dst in zip(cast_in, cast_out):
        val = src[...] if gain is None else src[...] * gain[...]
        dst[...] = val.astype(dst.dtype)


def _add_casts(casts, grid, in_specs, args, out_specs, out_shape):
    nbytes = 0
    for job in casts:
        plan = _plan_cast(job, grid)
        in_specs.extend(plan.in_specs)
        args.extend(plan.args)
        out_specs.append(plan.out_spec)
        out_shape.append(plan.out_shape)
        nbytes += plan.nbytes
    return nbytes


def _rmsnorm_kernel(x_ref, g_ref, o_ref):
    o_ref[...] = _rmsnorm_rows(x_ref[...], g_ref[...]).astype(o_ref.dtype)


def rmsnorm(x, g, out_dtype, tm=512):
    m, d = x.shape
    blocks = 2 * (_nbytes((tm, d), x.dtype) + _nbytes((tm, d), out_dtype))
    return pl.pallas_call(
        _rmsnorm_kernel,
        grid=(m // tm,),
        in_specs=[pl.BlockSpec((tm, d), lambda i: (i, 0)), pl.BlockSpec((1, d), lambda i: (0, 0))],
        out_specs=pl.BlockSpec((tm, d), lambda i: (i, 0)),
        out_shape=jax.ShapeDtypeStruct((m, d), out_dtype),
        compiler_params=pltpu.CompilerParams(
            dimension_semantics=("arbitrary",), vmem_limit_bytes=_vmem_limit(blocks)),
        name="rmsnorm",
    )(x, g.reshape(1, d))


def _accumulate_row_stats(ss_ref, tile, first):
    part = _row_sumsq(tile)

    @pl.when(first)
    def _():
        ss_ref[...] = part

    @pl.when(jnp.logical_not(first))
    def _():
        ss_ref[...] += part


def _mm_kernel(*refs, relu2, residual, scaled, stats, side, cast_gains, inv_d, w_dims):
    it = iter(refs)
    x_ref, w_ref = next(it), next(it)
    res_ref = next(it) if residual else None
    ss_in_ref = next(it) if scaled else None
    side_w_ref = next(it) if side else None
    cast_in = _take_cast_refs(it, cast_gains)
    o_ref = next(it)
    ob_ref, ss_out_ref = (next(it), next(it)) if stats else (None, None)
    side_o_ref = next(it) if side else None
    cast_out = [next(it) for _ in cast_gains]

    j = pl.program_id(1)
    x = x_ref[...]
    acc = lax.dot_general(x, w_ref[...], w_dims, preferred_element_type=F32)
    if scaled:
        row_scale = lax.rsqrt(ss_in_ref[...] * inv_d + EPS)
        acc = acc * _lane_tile(row_scale, acc.shape[1])
    if relu2:
        acc = jnp.maximum(acc, 0.0)
        acc = acc * acc
    if residual:
        acc = res_ref[...] + acc
    o_ref[...] = acc.astype(o_ref.dtype)
    if stats:
        ob_ref[...] = acc.astype(ob_ref.dtype)
        _accumulate_row_stats(ss_out_ref, acc, j == 0)
    if side:
        @pl.when(j == 0)
        def _():
            extra = lax.dot_general(x, side_w_ref[...], w_dims, preferred_element_type=F32)
            side_o_ref[...] = extra * row_scale if scaled else extra
    _run_casts(cast_in, cast_out)


def matmul(x, w, out_dtype, *, n=None, w_transposed=False, relu2=False, residual=None, row_ss=None, stats=False,
           side_w=None, casts=(), tm=1024, tn=1024, name="matmul"):
    m, k = x.shape
    n = w.shape[0 if w_transposed else 1] if n is None else n
    tn = min(tn, n)
    assert m % tm == 0 and n % tn == 0
    grid = (m // tm, n // tn)
    row_block = pl.BlockSpec((tm, LANES), lambda i, j: (i, 0))
    tile = pl.BlockSpec((tm, tn), lambda i, j: (i, j))
    if w_transposed:
        w_spec, side_spec = pl.BlockSpec((tn, k), lambda i, j: (j, 0)), pl.BlockSpec((LANES, k), lambda i, j: (0, 0))
    else:
        w_spec, side_spec = pl.BlockSpec((k, tn), lambda i, j: (0, j)), pl.BlockSpec((k, LANES), lambda i, j: (0, 0))
    in_specs = [pl.BlockSpec((tm, k), lambda i, j: (i, 0)), w_spec]
    args = [x, w]
    blocks = _nbytes((tm, k), x.dtype) + _nbytes((k, tn), w.dtype) + _nbytes((tm, tn), out_dtype)
    if residual is not None:
        in_specs.append(tile)
        args.append(residual)
        blocks += _nbytes((tm, tn), residual.dtype)
    if row_ss is not None:
        in_specs.append(row_block)
        args.append(row_ss)
    if side_w is not None:
        in_specs.append(side_spec)
        args.append(side_w)
        blocks += _nbytes((k, LANES), side_w.dtype)
    out_specs = [tile]
    out_shape = [jax.ShapeDtypeStruct((m, n), out_dtype)]
    if stats:
        out_specs += [tile, row_block]
        out_shape += [jax.ShapeDtypeStruct((m, n), BF16), jax.ShapeDtypeStruct((m, LANES), F32)]
        blocks += _nbytes((tm, tn), BF16)
    if side_w is not None:
        out_specs.append(row_block)
        out_shape.append(jax.ShapeDtypeStruct((m, LANES), F32))
    blocks += _add_casts(casts, grid, in_specs, args, out_specs, out_shape)
    return pl.pallas_call(
        functools.partial(_mm_kernel, relu2=relu2, residual=residual is not None, scaled=row_ss is not None,
                          stats=stats, side=side_w is not None,
                          cast_gains=tuple(job.gain is not None for job in casts), inv_d=1.0 / k,
                          w_dims=(((1,), (1 if w_transposed else 0,)), ((), ()))),
        grid=grid,
        in_specs=in_specs,
        out_specs=out_specs,
        out_shape=out_shape,
        compiler_params=pltpu.CompilerParams(
            dimension_semantics=("arbitrary", "arbitrary"),
            vmem_limit_bytes=_vmem_limit(2 * blocks + _nbytes((tm, tn), F32))),
        name=name,
    )(*args)


def _mm_ktiled_kernel(x_ref, w_ref, res_ref, o_ref, *stat_refs):
    j, kk = pl.program_id(1), pl.program_id(2)

    @pl.when(kk == 0)
    def _():
        o_ref[...] = res_ref[...] + jnp.dot(x_ref[...], w_ref[...], preferred_element_type=F32)

    @pl.when(kk > 0)
    def _():
        o_ref[...] += jnp.dot(x_ref[...], w_ref[...], preferred_element_type=F32)

    if stat_refs:
        ob_ref, ss_ref = stat_refs

        @pl.when(kk == pl.num_programs(2) - 1)
        def _():
            out = o_ref[...]
            ob_ref[...] = out.astype(ob_ref.dtype)
            _accumulate_row_stats(ss_ref, out, j == 0)


def matmul_ktiled(x, w, residual, *, stats=False, tm=1024, tn=1024, tk=4096, name="matmul_ktiled"):
    m, k = x.shape
    n = w.shape[1]
    assert m % tm == 0 and n % tn == 0 and k % tk == 0
    tile = pl.BlockSpec((tm, tn), lambda i, j, kk: (i, j))
    out_specs = [tile]
    out_shape = [jax.ShapeDtypeStruct((m, n), F32)]
    blocks = (_nbytes((tm, tk), x.dtype) + _nbytes((tk, tn), w.dtype) + 2 * _nbytes((tm, tn), F32))
    if stats:
        out_specs += [tile, pl.BlockSpec((tm, LANES), lambda i, j, kk: (i, 0))]
        out_shape += [jax.ShapeDtypeStruct((m, n), BF16), jax.ShapeDtypeStruct((m, LANES), F32)]
        blocks += _nbytes((tm, tn), BF16)
    return pl.pallas_call(
        _mm_ktiled_kernel,
        grid=(m // tm, n // tn, k // tk),
        in_specs=[
            pl.BlockSpec((tm, tk), lambda i, j, kk: (i, kk)),
            pl.BlockSpec((tk, tn), lambda i, j, kk: (kk, j)),
            tile,
        ],
        out_specs=out_specs,
        out_shape=out_shape,
        compiler_params=pltpu.CompilerParams(
            dimension_semantics=("arbitrary", "arbitrary", "arbitrary"),
            vmem_limit_bytes=_vmem_limit(2 * blocks + _nbytes((tm, tn), F32))),
        name=name,
    )(x, w, residual)


def _pool_kernel(*refs, tm, seq, halo, cast_gains):
    it = iter(refs)
    x_ref, xp_ref, xn_ref, gmix_ref, gmlp_ref, w_ref, scale_ref = (next(it) for _ in range(7))
    cast_in = _take_cast_refs(it, cast_gains)
    h_ref, hn_ref = next(it), next(it)
    cast_out = [next(it) for _ in cast_gains]
    buf_ref = next(it)

    i = pl.program_id(1)
    nblk = pl.num_programs(1)
    gmix = gmix_ref[...]
    x = x_ref[0]
    d_model = x.shape[-1]
    group = d_model // len(POOL_WINDOWS)

    buf_ref[halo:halo + tm, :] = _rmsnorm_rows(x, gmix)
    buf_ref[0:halo, :] = jnp.where(i > 0, _rmsnorm_rows(xp_ref[0], gmix), 0.0)
    buf_ref[halo + tm:halo + tm + halo, :] = jnp.where(i < nblk - 1, _rmsnorm_rows(xn_ref[0], gmix), 0.0)

    pos = i * tm + lax.broadcasted_iota(jnp.int32, (tm, 1), 0)
    for gi, win in enumerate(POOL_WINDOWS):
        cols = slice(gi * group, (gi + 1) * group)
        half = win // 2
        acc = buf_ref[halo - half:halo - half + tm, cols]
        for off in range(-half + 1, half):
            acc = acc + buf_ref[halo + off:halo + off + tm, cols]
        count = (jnp.minimum(pos + half, seq) - jnp.maximum(pos - half, 0)).astype(F32)
        diff = acc * (1.0 / count) - buf_ref[halo:halo + tm, cols]
        y = jnp.dot(diff.astype(BF16), w_ref[gi], preferred_element_type=F32)
        h_ref[0, :, cols] = x[:, cols] + y * scale_ref[:, cols]

    hn_ref[0] = _rmsnorm_rows(h_ref[0], gmlp_ref[...]).astype(hn_ref.dtype)
    _run_casts(cast_in, cast_out)


def pool_layer(x, g_mix, g_mlp, pool_w, pool_scale, casts=(), tm=256):
    b, s, d = x.shape
    halo = SUBLANES
    assert max(POOL_WINDOWS) // 2 <= halo and tm % halo == 0 and s % tm == 0
    ng, gd, _ = pool_w.shape
    hb = tm // halo
    grid = (b, s // tm)
    row_block = pl.BlockSpec((1, tm, d), lambda bi, i: (bi, i, 0))
    vec = pl.BlockSpec((1, d), lambda bi, i: (0, 0))
    in_specs = [
        row_block,
        pl.BlockSpec((1, halo, d), lambda bi, i: (bi, jnp.maximum(i * hb - 1, 0), 0)),
        pl.BlockSpec((1, halo, d), lambda bi, i: (bi, jnp.minimum((i + 1) * hb, s // halo - 1), 0)),
        vec,
        vec,
        pl.BlockSpec((ng, gd, gd), lambda bi, i: (0, 0, 0), pipeline_mode=pl.Buffered(1)),
        vec,
    ]
    args = [x, x, x, g_mix.reshape(1, d), g_mlp.reshape(1, d), pool_w.astype(BF16), pool_scale.reshape(1, d)]
    out_specs = [row_block, row_block]
    out_shape = [jax.ShapeDtypeStruct((b, s, d), F32), jax.ShapeDtypeStruct((b, s, d), BF16)]
    blocks = (2 * _nbytes((tm, d), F32) + _nbytes((tm, d), BF16)) * 2 + _nbytes(pool_w.shape, BF16)
    blocks += 2 * _add_casts(casts, grid, in_specs, args, out_specs, out_shape)
    scratch_bytes = _nbytes((tm + 2 * halo, d), F32)
    return pl.pallas_call(
        functools.partial(_pool_kernel, tm=tm, seq=s, halo=halo,
                          cast_gains=tuple(job.gain is not None for job in casts)),
        grid=grid,
        in_specs=in_specs,
        out_specs=out_specs,
        out_shape=out_shape,
        scratch_shapes=[pltpu.VMEM((tm + 2 * halo, d), F32)],
        compiler_params=pltpu.CompilerParams(
            dimension_semantics=("arbitrary", "arbitrary"),
            vmem_limit_bytes=_vmem_limit(blocks + scratch_bytes)),
        name="pool_layer",
    )(*args)


def _gla_block_stages(q_ref, k_ref, v_ref, r_ref, wup_ref, bias_ref, state_ref, finish, *, head, row0, rows, dk, dv,
                       reverse):
    c = CHUNK
    ns = rows // c
    order = tuple(reversed(range(ns))) if reverse else tuple(range(ns))
    kcols = slice(head * dk, (head + 1) * dk)
    vcols = slice(head * dv, (head + 1) * dv)

    def load(ref, cols):
        return jnp.concatenate([ref[0, row0 + ci * c:row0 + (ci + 1) * c, cols] for ci in order], axis=0)

    q = load(q_ref, kcols) * (dk ** -0.5)
    k = load(k_ref, kcols)
    z = jnp.dot(load(r_ref, slice(None)).astype(BF16), wup_ref[0, :, kcols], preferred_element_type=F32)
    g = jax.nn.log_sigmoid(z + bias_ref[0, :, kcols]) * (math.log2(math.e) / GATE_TAU)

    row = lax.broadcasted_iota(jnp.int32, (rows, rows), 0)
    col = lax.broadcasted_iota(jnp.int32, (rows, rows), 1)
    shift = c.bit_length() - 1
    row_chunk = lax.shift_right_logical(row, shift)
    col_chunk = lax.shift_right_logical(col, shift)
    causal = (col >= row) if reverse else (col <= row)
    same_chunk = (row_chunk == col_chunk) & causal
    tri = jnp.where(same_chunk, 1.0, 0.0).astype(BF16)
    g_hi = g.astype(BF16)
    g_lo = (g - g_hi.astype(F32)).astype(BF16)
    b = jnp.dot(tri, g_hi, preferred_element_type=F32) + jnp.dot(tri, g_lo, preferred_element_type=F32)

    last_i, mid_i = (0, c - c // 2) if reverse else (c - 1, c // 2 - 1)
    b_last = [b[p * c + last_i:p * c + last_i + 1, :] for p in range(ns)]
    b_mid = [b[p * c + mid_i:p * c + mid_i + 1, :] for p in range(ns)]
    a = [jnp.exp2(x) for x in b_last]
    yield False

    def prod(factors):
        out = None
        for f in factors:
            out = f if out is None else out * f
        return out

    q_intra, k_intra, q_inter, k_state, q_prev, k_next = [], [], [], [], [], []
    q_skip = {d: [] for d in range(2, ns)}
    for p in range(ns):
        rs = slice(p * c, (p + 1) * c)
        bp, qp, kp = b[rs], q[rs], k[rs]
        qn = qp * jnp.exp2(bp)
        q_intra.append((qp * jnp.exp2(bp - b_mid[p])).astype(BF16))
        q_inter.append(qn.astype(BF16))
        before = prod(a[:p])
        q_prev.append((qn if before is None else qn * before).astype(BF16))
        for d in range(2, p + 1):
            q_skip[d].append((qn * prod(a[p - d + 1:p])).astype(BF16))
        yield False
        ks = kp * jnp.exp2(b_last[p] - bp)
        k_intra.append((kp * jnp.exp2(b_mid[p] - bp)).astype(BF16))
        k_state.append(ks.astype(BF16))
        after = prod(a[p + 1:])
        k_next.append((ks if after is None else ks * after).astype(BF16))
        yield False
    v = load(v_ref, vcols).astype(BF16)
    yield True

    nt = (((1,), (1,)), ((), ()))
    s_intra = lax.dot_general(jnp.concatenate(q_intra, axis=0), jnp.concatenate(k_intra, axis=0), nt,
                              preferred_element_type=F32)
    yield None
    cross_lhs = q_inter + [x for d in range(2, ns) for x in q_skip[d]]
    s_cross = lax.dot_general(jnp.concatenate(cross_lhs, axis=0), jnp.concatenate(k_state, axis=0), nt,
                              preferred_element_type=F32)
    yield None

    skip_base = {}
    base = ns * c
    for d in range(2, ns):
        skip_base[d] = base
        base += (ns - d) * c
    score_rows = []
    row_in = lax.broadcasted_iota(jnp.int32, (c, rows), 0)
    col_in = lax.broadcasted_iota(jnp.int32, (c, rows), 1)
    col_chunk_in = lax.shift_right_logical(col_in, shift)
    for p in range(ns):
        rs = slice(p * c, (p + 1) * c)
        col_local = col_in - p * c
        causal_in = (col_local >= row_in) if reverse else (col_local <= row_in)
        sc = jnp.where((col_chunk_in == p) & causal_in, s_intra[rs], 0.0)
        if p >= 1:
            sc = jnp.where(col_chunk_in == p - 1, s_cross[rs], sc)
        for d in range(2, p + 1):
            off = skip_base[d] + (p - d) * c
            sc = jnp.where(col_chunk_in == p - d, s_cross[off:off + c], sc)
        score_rows.append(sc.astype(BF16))
    scores = jnp.concatenate(score_rows, axis=0)
    o = jnp.dot(scores, v, preferred_element_type=F32)
    yield None

    state = state_ref[head]
    o = o + jnp.dot(jnp.concatenate(q_prev, axis=0), state.astype(BF16), preferred_element_type=F32)
    yield None

    upd = lax.dot_general(jnp.concatenate(k_next, axis=0), v, (((0,), (0,)), ((), ())),
                          preferred_element_type=F32)
    decay_t = jnp.transpose(jnp.broadcast_to(prod(a), (LANES, dk)))
    state_ref[head] = _lane_tile(decay_t, dv) * state + upd
    yield None

    for p, ci in enumerate(order):
        finish(row0 + ci * c, o[p * c:(p + 1) * c])
        yield None


def _run_staggered(blocks):
    prev = None
    for gen in blocks:
        ready = False
        while not ready:
            ready = next(gen)
            if prev is not None:
                next(prev, None)
        if prev is not None:
            for _ in prev:
                pass
        prev = gen
    for _ in prev:
        pass


def _gla_kernel(*refs, nb, heads, sub, dk, dv, cast_gains):
    it = iter(refs)
    q_ref, k_ref, v_ref, r_ref, wup_ref, bias_ref, gate_ref, gnorm_ref = (next(it) for _ in range(8))
    cast_in = _take_cast_refs(it, cast_gains)
    o_ref = next(it)
    cast_out = [next(it) for _ in cast_gains]
    state_ref, acc_ref = next(it), next(it)

    s = pl.program_id(2)
    rows = q_ref.shape[1]
    stages = functools.partial(_gla_block_stages, q_ref, k_ref, v_ref, r_ref, wup_ref, bias_ref, state_ref,
                               rows=sub, dk=dk, dv=dv)
    starts = tuple(range(0, rows, sub))

    @pl.when((s == 0) | (s == nb))
    def _():
        state_ref[...] = jnp.zeros_like(state_ref)

    @pl.when(s < nb)
    def _():
        base = pl.multiple_of(s * rows, rows)

        def keep(head, row, piece):
            acc_ref[pl.ds(base + row, CHUNK), head * dv:(head + 1) * dv] = piece

        _run_staggered([stages(functools.partial(keep, head), head=head, row0=row0, reverse=False)
                        for head in range(heads) for row0 in starts])

    @pl.when(s >= nb)
    def _():
        base = pl.multiple_of((2 * nb - 1 - s) * rows, rows)
        gnorm = gnorm_ref[...]

        def emit(head, row, piece):
            vcols = slice(head * dv, (head + 1) * dv)
            o = _rmsnorm_rows(acc_ref[pl.ds(base + row, CHUNK), vcols] + piece, gnorm)
            gate = gate_ref[0, row:row + CHUNK, vcols]
            o_ref[0, row:row + CHUNK, vcols] = (o * jax.nn.silu(gate)).astype(o_ref.dtype)

        _run_staggered([stages(functools.partial(emit, head), head=head, row0=row0, reverse=True)
                        for head in range(heads) for row0 in reversed(starts)])

    _run_casts(cast_in, cast_out)


def gla_core(p, r, wup, bias, g_norm, *, heads, casts=(), bs=1024, sub=256, heads_per_step=1):
    b, s, width = p.shape
    dv_total = width // 3
    dk_total = dv_total // 2
    dk, dv = dk_total // heads, dv_total // heads
    hp = heads_per_step
    bs = min(bs, s)
    assert heads % hp == 0 and s % bs == 0 and bs % sub == 0 and sub % CHUNK == 0
    gk, gv = hp * dk, hp * dv
    nb = s // bs
    kq = dk_total // gk
    kv = (2 * dk_total) // gv
    kg = (2 * dk_total + dv_total) // gv
    rw = r.shape[-1]
    grid = (b, heads // hp, 2 * nb)

    def blk(i):
        return jnp.where(i < nb, i, 2 * nb - 1 - i)

    def out_blk(i):
        return jnp.where(i < nb, nb - 1, 2 * nb - 1 - i)

    in_specs = [
        pl.BlockSpec((1, bs, gk), lambda bi, h, i: (bi, blk(i), h)),
        pl.BlockSpec((1, bs, gk), lambda bi, h, i: (bi, blk(i), kq + h)),
        pl.BlockSpec((1, bs, gv), lambda bi, h, i: (bi, blk(i), kv + h)),
        pl.BlockSpec((1, bs, rw), lambda bi, h, i: (bi, blk(i), 0)),
        pl.BlockSpec((1, rw, gk), lambda bi, h, i: (i // nb, 0, h)),
        pl.BlockSpec((1, 1, gk), lambda bi, h, i: (i // nb, 0, h)),
        pl.BlockSpec((1, bs, gv), lambda bi, h, i: (bi, out_blk(i), kg + h)),
        pl.BlockSpec((1, dv), lambda bi, h, i: (0, 0)),
    ]
    args = [p, p, p, r, wup, bias, p, g_norm.reshape(1, dv)]
    out_specs = [pl.BlockSpec((1, bs, gv), lambda bi, h, i: (bi, out_blk(i), h))]
    out_shape = [jax.ShapeDtypeStruct((b, s, dv_total), BF16)]
    blocks = (2 * _nbytes((bs, gk), F32) + 2 * _nbytes((bs, gv), F32) + _nbytes((bs, rw), F32)
              + _nbytes((bs, gv), BF16))
    blocks += _add_casts(casts, grid, in_specs, args, out_specs, out_shape)
    scratch_bytes = _nbytes((hp, dk, dv), F32) + _nbytes((s, gv), F32)
    return pl.pallas_call(
        functools.partial(_gla_kernel, nb=nb, heads=hp, sub=sub, dk=dk, dv=dv,
                          cast_gains=tuple(job.gain is not None for job in casts)),
        grid=grid,
        in_specs=in_specs,
        out_specs=out_specs,
        out_shape=out_shape,
        scratch_shapes=[pltpu.VMEM((hp, dk, dv), F32), pltpu.VMEM((s, gv), F32)],
        compiler_params=pltpu.CompilerParams(
            dimension_semantics=("arbitrary", "arbitrary", "arbitrary"),
            vmem_limit_bytes=_vmem_limit(2 * blocks + scratch_bytes)),
        name="gla_scan",
    )(*args)


def gla_layer(h, g_mix, w_in_t, w_out_f32, layer, w_up_f, b_up_f, w_up_b, b_up_b, g_norm, next_cast):
    b, s, d = h.shape
    t = b * s
    main = w_in_t.shape[0] - 2 * GATE_RANK
    h2 = h.reshape(t, d)
    hn = rmsnorm(h2, g_mix, BF16)
    w_r_t = jnp.pad(w_in_t[main:], ((0, LANES - 2 * GATE_RANK), (0, 0)))
    p, r, w_out = matmul(hn, w_in_t, F32, n=main, w_transposed=True, side_w=w_r_t,
                         casts=(CastJob(w_out_f32, layer),), name="gla_in_proj")
    wup = jnp.stack([jnp.pad(w_up_f, ((0, LANES - GATE_RANK), (0, 0))),
                     jnp.pad(w_up_b, ((GATE_RANK, LANES - 2 * GATE_RANK), (0, 0)))]).astype(BF16)
    bias = jnp.stack([b_up_f, b_up_b])[:, None, :]
    og, next_bf16 = gla_core(p.reshape(b, s, main), r.reshape(b, s, LANES), wup, bias, g_norm, heads=GLA_HEADS,
                             casts=(next_cast,))
    out, out_b, out_ss = matmul(og.reshape(t, -1), w_out, F32, residual=h2, stats=True, tn=512,
                                name="gla_out_proj")
    return out, out_b, out_ss, next_bf16


def kernel(x, norm_mix, norm_mlp, norm_final, pool_w, pool_scale, gla_w_in, gla_w_up_f, gla_b_up_f, gla_w_up_b,
           gla_b_up_b, gla_g_norm, gla_w_out, mlp_w_in, mlp_w_out):
    b, s, d = x.shape
    t = b * s
    w_in_t_f32 = jnp.swapaxes(gla_w_in, 1, 2)
    h, hn, w1 = pool_layer(x, norm_mix[0], norm_mlp[0], pool_w[0], pool_scale[0], casts=(CastJob(mlp_w_in, 0),))
    a, w2, w_in = matmul(hn.reshape(t, d), w1, BF16, relu2=True,
                         casts=(CastJob(mlp_w_out, 0), CastJob(w_in_t_f32, 0)),
                         name="mlp_up_proj")
    (h,) = matmul_ktiled(a, w2, h.reshape(t, d), name="mlp_down_proj")

    h, hb, h_ss, w1 = gla_layer(h.reshape(b, s, d), norm_mix[1], w_in, gla_w_out, 0, gla_w_up_f[0], gla_b_up_f[0],
                                gla_w_up_b[0], gla_b_up_b[0], gla_g_norm[0],
                                CastJob(mlp_w_in, 1, gain=norm_mlp[1]))

    a, w2 = matmul(hb, w1, BF16, relu2=True, row_ss=h_ss, casts=(CastJob(mlp_w_out, 1),), name="mlp_up_proj")
    (h,) = matmul_ktiled(a, w2, h, name="mlp_down_proj")
    return rmsnorm(h, norm_final, F32).reshape(b, s, d)
```

```python
import functools
import math
from typing import NamedTuple, Optional

import jax
import jax.numpy as jnp
from jax import lax
from jax.experimental import pallas as pl
from jax.experimental.pallas import tpu as pltpu

EPS = 1e-6
POOL_WINDOWS = (2, 4, 8, 16)
GLA_HEADS = 4
GATE_RANK = 16
GATE_TAU = 16.0
CHUNK = 64

V7X_VMEM_BYTES = 64 * 1024 * 1024
LANES = 128
SUBLANES = 8
MXU_COLS = 256

F32 = jnp.float32
BF16 = jnp.bfloat16


def _vmem_limit(block_bytes):
    return int(min(V7X_VMEM_BYTES - 4 * 1024 * 1024, 2 * block_bytes))


def _nbytes(shape, dtype):
    n = 1
    for s in shape:
        n *= s
    return n * jnp.dtype(dtype).itemsize


def _lane_tile(x, n):
    return jnp.concatenate([x] * (n // LANES), axis=1)


def _row_sumsq(x):
    return jnp.broadcast_to(jnp.sum(x * x, axis=-1, keepdims=True), (x.shape[0], LANES))


def _rmsnorm_rows(x, g):
    y = x * lax.rsqrt(jnp.mean(x * x, axis=-1, keepdims=True) + EPS)
    return y * g


class CastJob(NamedTuple):
    src: jax.Array
    layer: int
    gain: Optional[jax.Array] = None
    gain_axis: int = 0


class _CastPlan(NamedTuple):
    in_specs: list
    args: list
    out_spec: pl.BlockSpec
    out_shape: jax.ShapeDtypeStruct
    nbytes: int


def _plan_cast(job, grid):
    _, r, c = job.src.shape
    steps = 1
    for g in grid:
        steps *= g
    bf16_rows = 2 * SUBLANES
    nslabs = next(n for n in range(min(steps, r // bf16_rows), 0, -1) if r % (n * bf16_rows) == 0)
    slab = r // nslabs

    def slab_of(*idx):
        step = 0
        for g, i in zip(grid, idx):
            step = step * g + i
        return step * nslabs // steps

    layer = job.layer
    in_specs = [pl.BlockSpec((None, slab, c), lambda *idx: (layer, slab_of(*idx), 0))]
    args = [job.src]
    if job.gain is not None and job.gain_axis == 0:
        in_specs.append(pl.BlockSpec((slab, 1), lambda *idx: (slab_of(*idx), 0)))
        args.append(job.gain.reshape(r, 1))
    elif job.gain is not None:
        in_specs.append(pl.BlockSpec((1, c), lambda *idx: (0, 0)))
        args.append(job.gain.reshape(1, c))
    return _CastPlan(in_specs, args, pl.BlockSpec((slab, c), lambda *idx: (slab_of(*idx), 0)),
                     jax.ShapeDtypeStruct((r, c), BF16), _nbytes((slab, c), F32) + _nbytes((slab, c), BF16))


def _take_cast_refs(it, gains):
    return [(next(it), next(it) if g else None) for g in gains]


def _run_casts(cast_in, cast_out):
    for (src, gain), dst in zip(cast_in, cast_out):
        val = src[...] if gain is None else src[...] * gain[...]
        dst[...] = val.astype(dst.dtype)


def _add_casts(casts, grid, in_specs, args, out_specs, out_shape):
    nbytes = 0
    for job in casts:
        plan = _plan_cast(job, grid)
        in_specs.extend(plan.in_specs)
        args.extend(plan.args)
        out_specs.append(plan.out_spec)
        out_shape.append(plan.out_shape)
        nbytes += plan.nbytes
    return nbytes


def _rmsnorm_kernel(x_ref, g_ref, o_ref):
    o_ref[...] = _rmsnorm_rows(x_ref[...], g_ref[...]).astype(o_ref.dtype)


def rmsnorm(x, g, out_dtype, tm=512):
    m, d = x.shape
    blocks = 2 * (_nbytes((tm, d), x.dtype) + _nbytes((tm, d), out_dtype))
    return pl.pallas_call(
        _rmsnorm_kernel,
        grid=(m // tm,),
        in_specs=[pl.BlockSpec((tm, d), lambda i: (i, 0)), pl.BlockSpec((1, d), lambda i: (0, 0))],
        out_specs=pl.BlockSpec((tm, d), lambda i: (i, 0)),
        out_shape=jax.ShapeDtypeStruct((m, d), out_dtype),
        compiler_params=pltpu.CompilerParams(
            dimension_semantics=("arbitrary",), vmem_limit_bytes=_vmem_limit(blocks)),
        name="rmsnorm",
    )(x, g.reshape(1, d))


def _accumulate_row_stats(ss_ref, part, first):
    @pl.when(first)
    def _():
        ss_ref[...] = part

    @pl.when(jnp.logical_not(first))
    def _():
        ss_ref[...] += part


def _mm_kernel(*refs, relu2, residual, scaled, stats, side, cast_gains, inv_d, w_dims):
    it = iter(refs)
    x_ref, w_ref = next(it), next(it)
    res_ref = next(it) if residual else None
    ss_in_ref = next(it) if scaled else None
    side_w_ref = next(it) if side else None
    cast_in = _take_cast_refs(it, cast_gains)
    o_ref = next(it)
    ob_ref, ss_out_ref = (next(it), next(it)) if stats else (None, None)
    side_o_ref = next(it) if side else None
    cast_out = [next(it) for _ in cast_gains]

    j = pl.program_id(1)
    w_transposed = w_dims[0][1][0] == 1
    row_scale = lax.rsqrt(ss_in_ref[...] * inv_d + EPS) if scaled else None
    ss_part = None
    width = MXU_COLS
    for c0 in range(0, o_ref.shape[1], width):
        cols = slice(c0, c0 + width)
        w_tile = w_ref[cols, :] if w_transposed else w_ref[:, cols]
        acc = lax.dot_general(x_ref[...], w_tile, w_dims, preferred_element_type=F32)
        if scaled:
            acc = acc * _lane_tile(row_scale, width)
        if relu2:
            acc = jnp.maximum(acc, 0.0)
            acc = acc * acc
        if residual:
            acc = res_ref[:, cols] + acc
        o_ref[:, cols] = acc.astype(o_ref.dtype)
        if stats:
            ob_ref[:, cols] = acc.astype(ob_ref.dtype)
            part = _row_sumsq(acc)
            ss_part = part if ss_part is None else ss_part + part
    if stats:
        _accumulate_row_stats(ss_out_ref, ss_part, j == 0)
    if side:
        @pl.when(j == 0)
        def _():
            extra = lax.dot_general(x_ref[...], side_w_ref[...], w_dims, preferred_element_type=F32)
            side_o_ref[...] = extra * row_scale if scaled else extra
    _run_casts(cast_in, cast_out)


def matmul(x, w, out_dtype, *, n=None, w_transposed=False, relu2=False, residual=None, row_ss=None, stats=False,
           side_w=None, casts=(), tm=1024, tn=1024, name="matmul"):
    m, k = x.shape
    n = w.shape[0 if w_transposed else 1] if n is None else n
    tn = min(tn, n)
    assert m % tm == 0 and n % tn == 0
    grid = (m // tm, n // tn)
    row_block = pl.BlockSpec((tm, LANES), lambda i, j: (i, 0))
    tile = pl.BlockSpec((tm, tn), lambda i, j: (i, j))
    if w_transposed:
        w_spec, side_spec = pl.BlockSpec((tn, k), lambda i, j: (j, 0)), pl.BlockSpec((LANES, k), lambda i, j: (0, 0))
    else:
        w_spec, side_spec = pl.BlockSpec((k, tn), lambda i, j: (0, j)), pl.BlockSpec((k, LANES), lambda i, j: (0, 0))
    in_specs = [pl.BlockSpec((tm, k), lambda i, j: (i, 0)), w_spec]
    args = [x, w]
    blocks = _nbytes((tm, k), x.dtype) + _nbytes((k, tn), w.dtype) + _nbytes((tm, tn), out_dtype)
    if residual is not None:
        in_specs.append(tile)
        args.append(residual)
        blocks += _nbytes((tm, tn), residual.dtype)
    if row_ss is not None:
        in_specs.append(row_block)
        args.append(row_ss)
    if side_w is not None:
        in_specs.append(side_spec)
        args.append(side_w)
        blocks += _nbytes((k, LANES), side_w.dtype)
    out_specs = [tile]
    out_shape = [jax.ShapeDtypeStruct((m, n), out_dtype)]
    if stats:
        out_specs += [tile, row_block]
        out_shape += [jax.ShapeDtypeStruct((m, n), BF16), jax.ShapeDtypeStruct((m, LANES), F32)]
        blocks += _nbytes((tm, tn), BF16)
    if side_w is not None:
        out_specs.append(row_block)
        out_shape.append(jax.ShapeDtypeStruct((m, LANES), F32))
    blocks += _add_casts(casts, grid, in_specs, args, out_specs, out_shape)
    return pl.pallas_call(
        functools.partial(_mm_kernel, relu2=relu2, residual=residual is not None, scaled=row_ss is not None,
                          stats=stats, side=side_w is not None,
                          cast_gains=tuple(job.gain is not None for job in casts), inv_d=1.0 / k,
                          w_dims=(((1,), (1 if w_transposed else 0,)), ((), ()))),
        grid=grid,
        in_specs=in_specs,
        out_specs=out_specs,
        out_shape=out_shape,
        compiler_params=pltpu.CompilerParams(
            dimension_semantics=("arbitrary", "arbitrary"),
            vmem_limit_bytes=_vmem_limit(2 * blocks + _nbytes((tm, tn), F32))),
        name=name,
    )(*args)


def _mm_ktiled_kernel(x_ref, w_ref, res_ref, o_ref, *stat_refs):
    j, kk = pl.program_id(1), pl.program_id(2)
    col_tiles = [slice(c0, c0 + MXU_COLS) for c0 in range(0, o_ref.shape[1], MXU_COLS)]

    @pl.when(kk == 0)
    def _():
        for cols in col_tiles:
            o_ref[:, cols] = res_ref[:, cols] + jnp.dot(x_ref[...], w_ref[:, cols], preferred_element_type=F32)

    @pl.when(kk > 0)
    def _():
        for cols in col_tiles:
            o_ref[:, cols] += jnp.dot(x_ref[...], w_ref[:, cols], preferred_element_type=F32)

    if stat_refs:
        ob_ref, ss_ref = stat_refs

        @pl.when(kk == pl.num_programs(2) - 1)
        def _():
            out = o_ref[...]
            ob_ref[...] = out.astype(ob_ref.dtype)
            _accumulate_row_stats(ss_ref, _row_sumsq(out), j == 0)


def matmul_ktiled(x, w, residual, *, stats=False, tm=1024, tn=1024, tk=4096, name="matmul_ktiled"):
    m, k = x.shape
    n = w.shape[1]
    assert m % tm == 0 and n % tn == 0 and k % tk == 0
    tile = pl.BlockSpec((tm, tn), lambda i, j, kk: (i, j))
    out_specs = [tile]
    out_shape = [jax.ShapeDtypeStruct((m, n), F32)]
    blocks = (_nbytes((tm, tk), x.dtype) + _nbytes((tk, tn), w.dtype) + 2 * _nbytes((tm, tn), F32))
    if stats:
        out_specs += [tile, pl.BlockSpec((tm, LANES), lambda i, j, kk: (i, 0))]
        out_shape += [jax.ShapeDtypeStruct((m, n), BF16), jax.ShapeDtypeStruct((m, LANES), F32)]
        blocks += _nbytes((tm, tn), BF16)
    return pl.pallas_call(
        _mm_ktiled_kernel,
        grid=(m // tm, n // tn, k // tk),
        in_specs=[
            pl.BlockSpec((tm, tk), lambda i, j, kk: (i, kk)),
            pl.BlockSpec((tk, tn), lambda i, j, kk: (kk, j)),
            tile,
        ],
        out_specs=out_specs,
        out_shape=out_shape,
        compiler_params=pltpu.CompilerParams(
            dimension_semantics=("arbitrary", "arbitrary", "arbitrary"),
            vmem_limit_bytes=_vmem_limit(2 * blocks + _nbytes((tm, tn), F32))),
        name=name,
    )(x, w, residual)


def _pool_kernel(*refs, tm, seq, halo, cast_gains):
    it = iter(refs)
    x_ref, xp_ref, xn_ref, gmix_ref, gmlp_ref, w_ref, scale_ref = (next(it) for _ in range(7))
    cast_in = _take_cast_refs(it, cast_gains)
    h_ref, hn_ref = next(it), next(it)
    cast_out = [next(it) for _ in cast_gains]
    buf_ref = next(it)

    i = pl.program_id(1)
    nblk = pl.num_programs(1)
    gmix = gmix_ref[...]
    x = x_ref[0]
    d_model = x.shape[-1]
    group = d_model // len(POOL_WINDOWS)

    buf_ref[halo:halo + tm, :] = _rmsnorm_rows(x, gmix)
    buf_ref[0:halo, :] = jnp.where(i > 0, _rmsnorm_rows(xp_ref[0], gmix), 0.0)
    buf_ref[halo + tm:halo + tm + halo, :] = jnp.where(i < nblk - 1, _rmsnorm_rows(xn_ref[0], gmix), 0.0)

    pos = i * tm + lax.broadcasted_iota(jnp.int32, (tm, 1), 0)
    for gi, win in enumerate(POOL_WINDOWS):
        cols = slice(gi * group, (gi + 1) * group)
        half = win // 2
        acc = buf_ref[halo - half:halo - half + tm, cols]
        for off in range(-half + 1, half):
            acc = acc + buf_ref[halo + off:halo + off + tm, cols]
        count = (jnp.minimum(pos + half, seq) - jnp.maximum(pos - half, 0)).astype(F32)
        diff = acc * (1.0 / count) - buf_ref[halo:halo + tm, cols]
        y = jnp.dot(diff.astype(BF16), w_ref[gi], preferred_element_type=F32)
        h_ref[0, :, cols] = x[:, cols] + y * scale_ref[:, cols]

    hn_ref[0] = _rmsnorm_rows(h_ref[0], gmlp_ref[...]).astype(hn_ref.dtype)
    _run_casts(cast_in, cast_out)


def pool_layer(x, g_mix, g_mlp, pool_w, pool_scale, casts=(), tm=256):
    b, s, d = x.shape
    halo = SUBLANES
    assert max(POOL_WINDOWS) // 2 <= halo and tm % halo == 0 and s % tm == 0
    ng, gd, _ = pool_w.shape
    hb = tm // halo
    grid = (b, s // tm)
    row_block = pl.BlockSpec((1, tm, d), lambda bi, i: (bi, i, 0))
    vec = pl.BlockSpec((1, d), lambda bi, i: (0, 0))
    in_specs = [
        row_block,
        pl.BlockSpec((1, halo, d), lambda bi, i: (bi, jnp.maximum(i * hb - 1, 0), 0)),
        pl.BlockSpec((1, halo, d), lambda bi, i: (bi, jnp.minimum((i + 1) * hb, s // halo - 1), 0)),
        vec,
        vec,
        pl.BlockSpec((ng, gd, gd), lambda bi, i: (0, 0, 0), pipeline_mode=pl.Buffered(1)),
        vec,
    ]
    args = [x, x, x, g_mix.reshape(1, d), g_mlp.reshape(1, d), pool_w.astype(BF16), pool_scale.reshape(1, d)]
    out_specs = [row_block, row_block]
    out_shape = [jax.ShapeDtypeStruct((b, s, d), F32), jax.ShapeDtypeStruct((b, s, d), BF16)]
    blocks = (2 * _nbytes((tm, d), F32) + _nbytes((tm, d), BF16)) * 2 + _nbytes(pool_w.shape, BF16)
    blocks += 2 * _add_casts(casts, grid, in_specs, args, out_specs, out_shape)
    scratch_bytes = _nbytes((tm + 2 * halo, d), F32)
    return pl.pallas_call(
        functools.partial(_pool_kernel, tm=tm, seq=s, halo=halo,
                          cast_gains=tuple(job.gain is not None for job in casts)),
        grid=grid,
        in_specs=in_specs,
        out_specs=out_specs,
        out_shape=out_shape,
        scratch_shapes=[pltpu.VMEM((tm + 2 * halo, d), F32)],
        compiler_params=pltpu.CompilerParams(
            dimension_semantics=("arbitrary", "arbitrary"),
            vmem_limit_bytes=_vmem_limit(blocks + scratch_bytes)),
        name="pool_layer",
    )(*args)


def _gla_block_stages(q_ref, k_ref, v_ref, r_ref, wup_ref, bias_ref, state_ref, finish, *, head, row0, rows, dk, dv,
                       reverse):
    c = CHUNK
    ns = rows // c
    order = tuple(reversed(range(ns))) if reverse else tuple(range(ns))
    kcols = slice(head * dk, (head + 1) * dk)
    vcols = slice(head * dv, (head + 1) * dv)

    def load(ref, cols):
        return jnp.concatenate([ref[0, row0 + ci * c:row0 + (ci + 1) * c, cols] for ci in order], axis=0)

    q = load(q_ref, kcols) * (dk ** -0.5)
    k = load(k_ref, kcols)
    z = jnp.dot(load(r_ref, slice(None)).astype(BF16), wup_ref[0, :, kcols], preferred_element_type=F32)
    g = jax.nn.log_sigmoid(z + bias_ref[0, :, kcols]) * (math.log2(math.e) / GATE_TAU)

    row = lax.broadcasted_iota(jnp.int32, (rows, rows), 0)
    col = lax.broadcasted_iota(jnp.int32, (rows, rows), 1)
    shift = c.bit_length() - 1
    row_chunk = lax.shift_right_logical(row, shift)
    col_chunk = lax.shift_right_logical(col, shift)
    causal = (col >= row) if reverse else (col <= row)
    same_chunk = (row_chunk == col_chunk) & causal
    tri = jnp.where(same_chunk, 1.0, 0.0).astype(BF16)
    g_hi = g.astype(BF16)
    g_lo = (g - g_hi.astype(F32)).astype(BF16)
    b = jnp.dot(tri, g_hi, preferred_element_type=F32) + jnp.dot(tri, g_lo, preferred_element_type=F32)

    last_i, mid_i = (0, c - c // 2) if reverse else (c - 1, c // 2 - 1)
    b_last = [b[p * c + last_i:p * c + last_i + 1, :] for p in range(ns)]
    b_mid = [b[p * c + mid_i:p * c + mid_i + 1, :] for p in range(ns)]
    a = [jnp.exp2(x) for x in b_last]
    yield False

    def prod(factors):
        out = None
        for f in factors:
            out = f if out is None else out * f
        return out

    q_intra, k_intra, q_inter, k_state, q_prev, k_next = [], [], [], [], [], []
    q_skip = {d: [] for d in range(2, ns)}
    for p in range(ns):
        rs = slice(p * c, (p + 1) * c)
        bp, qp, kp = b[rs], q[rs], k[rs]
        qn = qp * jnp.exp2(bp)
        q_intra.append((qp * jnp.exp2(bp - b_mid[p])).astype(BF16))
        q_inter.append(qn.astype(BF16))
        before = prod(a[:p])
        q_prev.append((qn if before is None else qn * before).astype(BF16))
        for d in range(2, p + 1):
            q_skip[d].append((qn * prod(a[p - d + 1:p])).astype(BF16))
        yield False
        ks = kp * jnp.exp2(b_last[p] - bp)
        k_intra.append((kp * jnp.exp2(b_mid[p] - bp)).astype(BF16))
        k_state.append(ks.astype(BF16))
        after = prod(a[p + 1:])
        k_next.append((ks if after is None else ks * after).astype(BF16))
        yield False
    v = load(v_ref, vcols).astype(BF16)
    yield True

    nt = (((1,), (1,)), ((), ()))
    s_intra = lax.dot_general(jnp.concatenate(q_intra, axis=0), jnp.concatenate(k_intra, axis=0), nt,
                              preferred_element_type=F32)
    yield None
    cross_lhs = q_inter + [x for d in range(2, ns) for x in q_skip[d]]
    s_cross = lax.dot_general(jnp.concatenate(cross_lhs, axis=0), jnp.concatenate(k_state, axis=0), nt,
                              preferred_element_type=F32)
    yield None

    skip_base = {}
    base = ns * c
    for d in range(2, ns):
        skip_base[d] = base
        base += (ns - d) * c
    score_rows = []
    row_in = lax.broadcasted_iota(jnp.int32, (c, rows), 0)
    col_in = lax.broadcasted_iota(jnp.int32, (c, rows), 1)
    col_chunk_in = lax.shift_right_logical(col_in, shift)
    for p in range(ns):
        rs = slice(p * c, (p + 1) * c)
        col_local = col_in - p * c
        causal_in = (col_local >= row_in) if reverse else (col_local <= row_in)
        sc = jnp.where((col_chunk_in == p) & causal_in, s_intra[rs], 0.0)
        if p >= 1:
            sc = jnp.where(col_chunk_in == p - 1, s_cross[rs], sc)
        for d in range(2, p + 1):
            off = skip_base[d] + (p - d) * c
            sc = jnp.where(col_chunk_in == p - d, s_cross[off:off + c], sc)
        score_rows.append(sc.astype(BF16))
    scores = jnp.concatenate(score_rows, axis=0)
    o = jnp.dot(scores, v, preferred_element_type=F32)
    yield None

    state = state_ref[head]
    o = o + jnp.dot(jnp.concatenate(q_prev, axis=0), state.astype(BF16), preferred_element_type=F32)
    yield None

    upd = lax.dot_general(jnp.concatenate(k_next, axis=0), v, (((0,), (0,)), ((), ())),
                          preferred_element_type=F32)
    decay_t = jnp.transpose(jnp.broadcast_to(prod(a), (LANES, dk)))
    state_ref[head] = _lane_tile(decay_t, dv) * state + upd
    yield None

    for p, ci in enumerate(order):
        finish(row0 + ci * c, o[p * c:(p + 1) * c])
        yield None


def _run_staggered(blocks):
    prev = None
    for gen in blocks:
        ready = False
        while not ready:
            ready = next(gen)
            if prev is not None:
                next(prev, None)
        if prev is not None:
            for _ in prev:
                pass
        prev = gen
    for _ in prev:
        pass


def _gla_kernel(*refs, nb, heads, sub, dk, dv, cast_gains):
    it = iter(refs)
    q_ref, k_ref, v_ref, r_ref, wup_ref, bias_ref, gate_ref, gnorm_ref = (next(it) for _ in range(8))
    cast_in = _take_cast_refs(it, cast_gains)
    o_ref = next(it)
    cast_out = [next(it) for _ in cast_gains]
    state_ref, acc_ref = next(it), next(it)

    s = pl.program_id(2)
    rows = q_ref.shape[1]
    stages = functools.partial(_gla_block_stages, q_ref, k_ref, v_ref, r_ref, wup_ref, bias_ref, state_ref,
                               rows=sub, dk=dk, dv=dv)
    starts = tuple(range(0, rows, sub))

    @pl.when((s == 0) | (s == nb))
    def _():
        state_ref[...] = jnp.zeros_like(state_ref)

    @pl.when(s < nb)
    def _():
        base = pl.multiple_of(s * rows, rows)

        def keep(head, row, piece):
            acc_ref[pl.ds(base + row, CHUNK), head * dv:(head + 1) * dv] = piece

        _run_staggered([stages(functools.partial(keep, head), head=head, row0=row0, reverse=False)
                        for head in range(heads) for row0 in starts])

    @pl.when(s >= nb)
    def _():
        base = pl.multiple_of((2 * nb - 1 - s) * rows, rows)
        gnorm = gnorm_ref[...]

        def emit(head, row, piece):
            vcols = slice(head * dv, (head + 1) * dv)
            o = _rmsnorm_rows(acc_ref[pl.ds(base + row, CHUNK), vcols] + piece, gnorm)
            gate = gate_ref[0, row:row + CHUNK, vcols]
            o_ref[0, row:row + CHUNK, vcols] = (o * jax.nn.silu(gate)).astype(o_ref.dtype)

        _run_staggered([stages(functools.partial(emit, head), head=head, row0=row0, reverse=True)
                        for head in range(heads) for row0 in reversed(starts)])

    _run_casts(cast_in, cast_out)


def gla_core(p, r, wup, bias, g_norm, *, heads, casts=(), bs=1024, sub=256, heads_per_step=1):
    b, s, width = p.shape
    dv_total = width // 3
    dk_total = dv_total // 2
    dk, dv = dk_total // heads, dv_total // heads
    hp = heads_per_step
    bs = min(bs, s)
    assert heads % hp == 0 and s % bs == 0 and bs % sub == 0 and sub % CHUNK == 0
    gk, gv = hp * dk, hp * dv
    nb = s // bs
    kq = dk_total // gk
    kv = (2 * dk_total) // gv
    kg = (2 * dk_total + dv_total) // gv
    rw = r.shape[-1]
    grid = (b, heads // hp, 2 * nb)

    def blk(i):
        return jnp.where(i < nb, i, 2 * nb - 1 - i)

    def out_blk(i):
        return jnp.where(i < nb, nb - 1, 2 * nb - 1 - i)

    in_specs = [
        pl.BlockSpec((1, bs, gk), lambda bi, h, i: (bi, blk(i), h)),
        pl.BlockSpec((1, bs, gk), lambda bi, h, i: (bi, blk(i), kq + h)),
        pl.BlockSpec((1, bs, gv), lambda bi, h, i: (bi, blk(i), kv + h)),
        pl.BlockSpec((1, bs, rw), lambda bi, h, i: (bi, blk(i), 0)),
        pl.BlockSpec((1, rw, gk), lambda bi, h, i: (i // nb, 0, h)),
        pl.BlockSpec((1, 1, gk), lambda bi, h, i: (i // nb, 0, h)),
        pl.BlockSpec((1, bs, gv), lambda bi, h, i: (bi, out_blk(i), kg + h)),
        pl.BlockSpec((1, dv), lambda bi, h, i: (0, 0)),
    ]
    args = [p, p, p, r, wup, bias, p, g_norm.reshape(1, dv)]
    out_specs = [pl.BlockSpec((1, bs, gv), lambda bi, h, i: (bi, out_blk(i), h))]
    out_shape = [jax.ShapeDtypeStruct((b, s, dv_total), BF16)]
    blocks = (2 * _nbytes((bs, gk), F32) + 2 * _nbytes((bs, gv), F32) + _nbytes((bs, rw), F32)
              + _nbytes((bs, gv), BF16))
    blocks += _add_casts(casts, grid, in_specs, args, out_specs, out_shape)
    scratch_bytes = _nbytes((hp, dk, dv), F32) + _nbytes((s, gv), F32)
    return pl.pallas_call(
        functools.partial(_gla_kernel, nb=nb, heads=hp, sub=sub, dk=dk, dv=dv,
                          cast_gains=tuple(job.gain is not None for job in casts)),
        grid=grid,
        in_specs=in_specs,
        out_specs=out_specs,
        out_shape=out_shape,
        scratch_shapes=[pltpu.VMEM((hp, dk, dv), F32), pltpu.VMEM((s, gv), F32)],
        compiler_params=pltpu.CompilerParams(
            dimension_semantics=("arbitrary", "arbitrary", "arbitrary"),
            vmem_limit_bytes=_vmem_limit(2 * blocks + scratch_bytes)),
        name="gla_scan",
    )(*args)


def gla_layer(h, hb, h_ss, w_in_t, w_out_f32, layer, w_up_f, b_up_f, w_up_b, b_up_b, g_norm, next_cast):
    b, s, d = h.shape
    t = b * s
    main = w_in_t.shape[0] - 2 * GATE_RANK
    h2 = h.reshape(t, d)
    w_r_t = jnp.pad(w_in_t[main:], ((0, LANES - 2 * GATE_RANK), (0, 0)))
    p, r, w_out = matmul(hb, w_in_t, F32, n=main, w_transposed=True, row_ss=h_ss, side_w=w_r_t,
                         casts=(CastJob(w_out_f32, layer),), name="gla_in_proj")
    wup = jnp.stack([jnp.pad(w_up_f, ((0, LANES - GATE_RANK), (0, 0))),
                     jnp.pad(w_up_b, ((GATE_RANK, LANES - 2 * GATE_RANK), (0, 0)))]).astype(BF16)
    bias = jnp.stack([b_up_f, b_up_b])[:, None, :]
    og, next_bf16 = gla_core(p.reshape(b, s, main), r.reshape(b, s, LANES), wup, bias, g_norm, heads=GLA_HEADS,
                             casts=(next_cast,))
    out, out_b, out_ss = matmul(og.reshape(t, -1), w_out, F32, residual=h2, stats=True, name="gla_out_proj")
    return out, out_b, out_ss, next_bf16


def kernel(x, norm_mix, norm_mlp, norm_final, pool_w, pool_scale, gla_w_in, gla_w_up_f, gla_b_up_f, gla_w_up_b,
           gla_b_up_b, gla_g_norm, gla_w_out, mlp_w_in, mlp_w_out):
    b, s, d = x.shape
    t = b * s
    w_in_t_f32 = jnp.swapaxes(gla_w_in, 1, 2)
    h, hn, w1 = pool_layer(x, norm_mix[0], norm_mlp[0], pool_w[0], pool_scale[0], casts=(CastJob(mlp_w_in, 0),))
    a, w2, w_in = matmul(hn.reshape(t, d), w1, BF16, relu2=True,
                         casts=(CastJob(mlp_w_out, 0), CastJob(w_in_t_f32, 0, gain=norm_mix[1], gain_axis=1)),
                         name="mlp_up_proj")
    h, hb, h_ss = matmul_ktiled(a, w2, h.reshape(t, d), stats=True, name="mlp_down_proj")

    h, hb, h_ss, w1 = gla_layer(h.reshape(b, s, d), hb, h_ss, w_in, gla_w_out, 0, gla_w_up_f[0], gla_b_up_f[0],
                                gla_w_up_b[0], gla_b_up_b[0], gla_g_norm[0],
                                CastJob(mlp_w_in, 1, gain=norm_mlp[1]))

    a, w2 = matmul(hb, w1, BF16, relu2=True, row_ss=h_ss, casts=(CastJob(mlp_w_out, 1),), name="mlp_up_proj")
    (h,) = matmul_ktiled(a, w2, h, name="mlp_down_proj")
    return rmsnorm(h, norm_final, F32).reshape(b, s, d)
```

```python
import functools
import math
from typing import NamedTuple, Optional

import jax
import jax.numpy as jnp
from jax import lax
from jax.experimental import pallas as pl
from jax.experimental.pallas import tpu as pltpu

EPS = 1e-6
POOL_WINDOWS = (2, 4, 8, 16)
GLA_HEADS = 4
GATE_RANK = 16
GATE_TAU = 16.0
CHUNK = 64

V7X_VMEM_BYTES = 64 * 1024 * 1024
LANES = 128
SUBLANES = 8
MXU_COLS = 256

F32 = jnp.float32
BF16 = jnp.bfloat16


def _vmem_limit(block_bytes):
    return int(min(V7X_VMEM_BYTES - 4 * 1024 * 1024, 2 * block_bytes))


def _nbytes(shape, dtype):
    n = 1
    for s in shape:
        n *= s
    return n * jnp.dtype(dtype).itemsize


def _lane_tile(x, n):
    return jnp.concatenate([x] * (n // LANES), axis=1)


def _row_sumsq(x):
    return jnp.broadcast_to(jnp.sum(x * x, axis=-1, keepdims=True), (x.shape[0], LANES))


def _rmsnorm_rows(x, g):
    y = x * lax.rsqrt(jnp.mean(x * x, axis=-1, keepdims=True) + EPS)
    return y * g


class CastJob(NamedTuple):
    src: jax.Array
    layer: int
    gain: Optional[jax.Array] = None
    gain_axis: int = 0


class _CastPlan(NamedTuple):
    in_specs: list
    args: list
    out_spec: pl.BlockSpec
    out_shape: jax.ShapeDtypeStruct
    nbytes: int


def _plan_cast(job, grid):
    _, r, c = job.src.shape
    steps = 1
    for g in grid:
        steps *= g
    bf16_rows = 2 * SUBLANES
    nslabs = next(n for n in range(min(steps, r // bf16_rows), 0, -1) if r % (n * bf16_rows) == 0)
    slab = r // nslabs

    def slab_of(*idx):
        step = 0
        for g, i in zip(grid, idx):
            step = step * g + i
        return step * nslabs // steps

    layer = job.layer
    in_specs = [pl.BlockSpec((None, slab, c), lambda *idx: (layer, slab_of(*idx), 0))]
    args = [job.src]
    if job.gain is not None and job.gain_axis == 0:
        in_specs.append(pl.BlockSpec((slab, 1), lambda *idx: (slab_of(*idx), 0)))
        args.append(job.gain.reshape(r, 1))
    elif job.gain is not None:
        in_specs.append(pl.BlockSpec((1, c), lambda *idx: (0, 0)))
        args.append(job.gain.reshape(1, c))
    return _CastPlan(in_specs, args, pl.BlockSpec((slab, c), lambda *idx: (slab_of(*idx), 0)),
                     jax.ShapeDtypeStruct((r, c), BF16), _nbytes((slab, c), F32) + _nbytes((slab, c), BF16))


def _take_cast_refs(it, gains):
    return [(next(it), next(it) if g else None) for g in gains]


def _run_casts(cast_in, cast_out):
    for (src, gain), dst in zip(cast_in, cast_out):
        val = src[...] if gain is None else src[...] * gain[...]
        dst[...] = val.astype(dst.dtype)


def _add_casts(casts, grid, in_specs, args, out_specs, out_shape):
    nbytes = 0
    for job in casts:
        plan = _plan_cast(job, grid)
        in_specs.extend(plan.in_specs)
        args.extend(plan.args)
        out_specs.append(plan.out_spec)
        out_shape.append(plan.out_shape)
        nbytes += plan.nbytes
    return nbytes


def _rmsnorm_kernel(x_ref, g_ref, o_ref):
    o_ref[...] = _rmsnorm_rows(x_ref[...], g_ref[...]).astype(o_ref.dtype)


def rmsnorm(x, g, out_dtype, tm=512):
    m, d = x.shape
    blocks = 2 * (_nbytes((tm, d), x.dtype) + _nbytes((tm, d), out_dtype))
    return pl.pallas_call(
        _rmsnorm_kernel,
        grid=(m // tm,),
        in_specs=[pl.BlockSpec((tm, d), lambda i: (i, 0)), pl.BlockSpec((1, d), lambda i: (0, 0))],
        out_specs=pl.BlockSpec((tm, d), lambda i: (i, 0)),
        out_shape=jax.ShapeDtypeStruct((m, d), out_dtype),
        compiler_params=pltpu.CompilerParams(
            dimension_semantics=("arbitrary",), vmem_limit_bytes=_vmem_limit(blocks)),
        name="rmsnorm",
    )(x, g.reshape(1, d))


def _accumulate_row_stats(ss_ref, part, first):
    @pl.when(first)
    def _():
        ss_ref[...] = part

    @pl.when(jnp.logical_not(first))
    def _():
        ss_ref[...] += part


def _mm_kernel(*refs, relu2, residual, scaled, stats, side, cast_gains, inv_d, w_dims):
    it = iter(refs)
    x_ref, w_ref = next(it), next(it)
    res_ref = next(it) if residual else None
    ss_in_ref = next(it) if scaled else None
    side_w_ref = next(it) if side else None
    cast_in = _take_cast_refs(it, cast_gains)
    o_ref = next(it)
    ob_ref, ss_out_ref = (next(it), next(it)) if stats else (None, None)
    side_o_ref = next(it) if side else None
    cast_out = [next(it) for _ in cast_gains]

    j = pl.program_id(1)
    w_transposed = w_dims[0][1][0] == 1
    row_scale = lax.rsqrt(ss_in_ref[...] * inv_d + EPS) if scaled else None
    ss_part = None
    width = MXU_COLS
    for c0 in range(0, o_ref.shape[1], width):
        cols = slice(c0, c0 + width)
        w_tile = w_ref[cols, :] if w_transposed else w_ref[:, cols]
        acc = lax.dot_general(x_ref[...], w_tile, w_dims, preferred_element_type=F32)
        if scaled:
            acc = acc * _lane_tile(row_scale, width)
        if relu2:
            acc = jnp.maximum(acc, 0.0)
            acc = acc * acc
        if residual:
            acc = res_ref[:, cols] + acc
        o_ref[:, cols] = acc.astype(o_ref.dtype)
        if stats:
            ob_ref[:, cols] = acc.astype(ob_ref.dtype)
            part = _row_sumsq(acc)
            ss_part = part if ss_part is None else ss_part + part
    if stats:
        _accumulate_row_stats(ss_out_ref, ss_part, j == 0)
    if side:
        @pl.when(j == 0)
        def _():
            extra = lax.dot_general(x_ref[...], side_w_ref[...], w_dims, preferred_element_type=F32)
            side_o_ref[...] = extra * row_scale if scaled else extra
    _run_casts(cast_in, cast_out)


def matmul(x, w, out_dtype, *, n=None, w_transposed=False, relu2=False, residual=None, row_ss=None, stats=False,
           side_w=None, casts=(), tm=1024, tn=1024, name="matmul"):
    m, k = x.shape
    n = w.shape[0 if w_transposed else 1] if n is None else n
    tn = min(tn, n)
    assert m % tm == 0 and n % tn == 0
    grid = (m // tm, n // tn)
    row_block = pl.BlockSpec((tm, LANES), lambda i, j: (i, 0))
    tile = pl.BlockSpec((tm, tn), lambda i, j: (i, j))
    if w_transposed:
        w_spec, side_spec = pl.BlockSpec((tn, k), lambda i, j: (j, 0)), pl.BlockSpec((LANES, k), lambda i, j: (0, 0))
    else:
        w_spec, side_spec = pl.BlockSpec((k, tn), lambda i, j: (0, j)), pl.BlockSpec((k, LANES), lambda i, j: (0, 0))
    in_specs = [pl.BlockSpec((tm, k), lambda i, j: (i, 0)), w_spec]
    args = [x, w]
    blocks = _nbytes((tm, k), x.dtype) + _nbytes((k, tn), w.dtype) + _nbytes((tm, tn), out_dtype)
    if residual is not None:
        in_specs.append(tile)
        args.append(residual)
        blocks += _nbytes((tm, tn), residual.dtype)
    if row_ss is not None:
        in_specs.append(row_block)
        args.append(row_ss)
    if side_w is not None:
        in_specs.append(side_spec)
        args.append(side_w)
        blocks += _nbytes((k, LANES), side_w.dtype)
    out_specs = [tile]
    out_shape = [jax.ShapeDtypeStruct((m, n), out_dtype)]
    if stats:
        out_specs += [tile, row_block]
        out_shape += [jax.ShapeDtypeStruct((m, n), BF16), jax.ShapeDtypeStruct((m, LANES), F32)]
        blocks += _nbytes((tm, tn), BF16)
    if side_w is not None:
        out_specs.append(row_block)
        out_shape.append(jax.ShapeDtypeStruct((m, LANES), F32))
    blocks += _add_casts(casts, grid, in_specs, args, out_specs, out_shape)
    return pl.pallas_call(
        functools.partial(_mm_kernel, relu2=relu2, residual=residual is not None, scaled=row_ss is not None,
                          stats=stats, side=side_w is not None,
                          cast_gains=tuple(job.gain is not None for job in casts), inv_d=1.0 / k,
                          w_dims=(((1,), (1 if w_transposed else 0,)), ((), ()))),
        grid=grid,
        in_specs=in_specs,
        out_specs=out_specs,
        out_shape=out_shape,
        compiler_params=pltpu.CompilerParams(
            dimension_semantics=("arbitrary", "arbitrary"),
            vmem_limit_bytes=_vmem_limit(2 * blocks + _nbytes((tm, tn), F32))),
        name=name,
    )(*args)


def _mm_ktiled_kernel(x_ref, w_ref, res_ref, o_ref, *stat_refs):
    j, kk = pl.program_id(1), pl.program_id(2)
    col_tiles = [slice(c0, c0 + MXU_COLS) for c0 in range(0, o_ref.shape[1], MXU_COLS)]

    @pl.when(kk == 0)
    def _():
        for cols in col_tiles:
            o_ref[:, cols] = res_ref[:, cols] + jnp.dot(x_ref[...], w_ref[:, cols], preferred_element_type=F32)

    @pl.when(kk > 0)
    def _():
        for cols in col_tiles:
            o_ref[:, cols] += jnp.dot(x_ref[...], w_ref[:, cols], preferred_element_type=F32)

    if stat_refs:
        ob_ref, ss_ref = stat_refs

        @pl.when(kk == pl.num_programs(2) - 1)
        def _():
            out = o_ref[...]
            ob_ref[...] = out.astype(ob_ref.dtype)
            _accumulate_row_stats(ss_ref, _row_sumsq(out), j == 0)


def matmul_ktiled(x, w, residual, *, stats=False, tm=1024, tn=1024, tk=4096, name="matmul_ktiled"):
    m, k = x.shape
    n = w.shape[1]
    assert m % tm == 0 and n % tn == 0 and k % tk == 0
    tile = pl.BlockSpec((tm, tn), lambda i, j, kk: (i, j))
    out_specs = [tile]
    out_shape = [jax.ShapeDtypeStruct((m, n), F32)]
    blocks = (_nbytes((tm, tk), x.dtype) + _nbytes((tk, tn), w.dtype) + 2 * _nbytes((tm, tn), F32))
    if stats:
        out_specs += [tile, pl.BlockSpec((tm, LANES), lambda i, j, kk: (i, 0))]
        out_shape += [jax.ShapeDtypeStruct((m, n), BF16), jax.ShapeDtypeStruct((m, LANES), F32)]
        blocks += _nbytes((tm, tn), BF16)
    return pl.pallas_call(
        _mm_ktiled_kernel,
        grid=(m // tm, n // tn, k // tk),
        in_specs=[
            pl.BlockSpec((tm, tk), lambda i, j, kk: (i, kk)),
            pl.BlockSpec((tk, tn), lambda i, j, kk: (kk, j)),
            tile,
        ],
        out_specs=out_specs,
        out_shape=out_shape,
        compiler_params=pltpu.CompilerParams(
            dimension_semantics=("arbitrary", "arbitrary", "arbitrary"),
            vmem_limit_bytes=_vmem_limit(2 * blocks + _nbytes((tm, tn), F32))),
        name=name,
    )(x, w, residual)


def _pool_kernel(*refs, tm, seq, halo, cast_gains):
    it = iter(refs)
    x_ref, xp_ref, xn_ref, gmix_ref, gmlp_ref, w_ref, scale_ref = (next(it) for _ in range(7))
    cast_in = _take_cast_refs(it, cast_gains)
    h_ref, hn_ref = next(it), next(it)
    cast_out = [next(it) for _ in cast_gains]
    buf_ref, lvl_ref = next(it), next(it)

    i = pl.program_id(1)
    nblk = pl.num_programs(1)
    gmix = gmix_ref[...]
    x = x_ref[0]
    d_model = x.shape[-1]
    group = d_model // len(POOL_WINDOWS)

    ext = tm + 2 * halo
    buf_ref[halo:halo + tm, :] = _rmsnorm_rows(x, gmix)
    buf_ref[0:halo, :] = jnp.where(i > 0, _rmsnorm_rows(xp_ref[0], gmix), 0.0)
    buf_ref[halo + tm:ext, :] = jnp.where(i < nblk - 1, _rmsnorm_rows(xn_ref[0], gmix), 0.0)
    buf_ref[ext:ext + halo, :] = jnp.zeros((halo, d_model), F32)
    lvl_ref[:, ext:ext + halo, :] = jnp.zeros((2, halo, group), F32)

    pos = i * tm + lax.broadcasted_iota(jnp.int32, (tm, 1), 0)
    for gi, win in enumerate(POOL_WINDOWS):
        cols = slice(gi * group, (gi + 1) * group)
        half = win // 2
        load = lambda lo, n: buf_ref[lo:lo + n, cols]
        span, slot = 1, 0
        while span < half:
            lvl_ref[slot, 0:ext, :] = load(0, ext) + load(span, ext)
            load = functools.partial(lambda s, lo, n: lvl_ref[s, lo:lo + n, :], slot)
            span, slot = 2 * span, 1 - slot
        acc = load(halo - half, tm) + load(halo, tm)
        count = (jnp.minimum(pos + half, seq) - jnp.maximum(pos - half, 0)).astype(F32)
        diff = acc * (1.0 / count) - buf_ref[halo:halo + tm, cols]
        y = jnp.dot(diff.astype(BF16), w_ref[gi], preferred_element_type=F32)
        h_ref[0, :, cols] = x[:, cols] + y * scale_ref[:, cols]

    hn_ref[0] = _rmsnorm_rows(h_ref[0], gmlp_ref[...]).astype(hn_ref.dtype)
    _run_casts(cast_in, cast_out)


def pool_layer(x, g_mix, g_mlp, pool_w, pool_scale, casts=(), tm=256):
    b, s, d = x.shape
    halo = SUBLANES
    assert max(POOL_WINDOWS) // 2 <= halo and tm % halo == 0 and s % tm == 0
    ng, gd, _ = pool_w.shape
    hb = tm // halo
    grid = (b, s // tm)
    row_block = pl.BlockSpec((1, tm, d), lambda bi, i: (bi, i, 0))
    vec = pl.BlockSpec((1, d), lambda bi, i: (0, 0))
    in_specs = [
        row_block,
        pl.BlockSpec((1, halo, d), lambda bi, i: (bi, jnp.maximum(i * hb - 1, 0), 0)),
        pl.BlockSpec((1, halo, d), lambda bi, i: (bi, jnp.minimum((i + 1) * hb, s // halo - 1), 0)),
        vec,
        vec,
        pl.BlockSpec((ng, gd, gd), lambda bi, i: (0, 0, 0), pipeline_mode=pl.Buffered(1)),
        vec,
    ]
    args = [x, x, x, g_mix.reshape(1, d), g_mlp.reshape(1, d), pool_w.astype(BF16), pool_scale.reshape(1, d)]
    out_specs = [row_block, row_block]
    out_shape = [jax.ShapeDtypeStruct((b, s, d), F32), jax.ShapeDtypeStruct((b, s, d), BF16)]
    blocks = (2 * _nbytes((tm, d), F32) + _nbytes((tm, d), BF16)) * 2 + _nbytes(pool_w.shape, BF16)
    blocks += 2 * _add_casts(casts, grid, in_specs, args, out_specs, out_shape)
    scratch = [pltpu.VMEM((tm + 3 * halo, d), F32), pltpu.VMEM((2, tm + 3 * halo, gd), F32)]
    scratch_bytes = _nbytes((tm + 3 * halo, d + 2 * gd), F32)
    return pl.pallas_call(
        functools.partial(_pool_kernel, tm=tm, seq=s, halo=halo,
                          cast_gains=tuple(job.gain is not None for job in casts)),
        grid=grid,
        in_specs=in_specs,
        out_specs=out_specs,
        out_shape=out_shape,
        scratch_shapes=scratch,
        compiler_params=pltpu.CompilerParams(
            dimension_semantics=("arbitrary", "arbitrary"),
            vmem_limit_bytes=_vmem_limit(blocks + scratch_bytes)),
        name="pool_layer",
    )(*args)


def _gla_block_stages(q_ref, k_ref, v_ref, r_ref, wup_ref, bias_ref, state_ref, finish, *, head, row0, rows, dk, dv,
                       reverse):
    c = CHUNK
    ns = rows // c
    order = tuple(reversed(range(ns))) if reverse else tuple(range(ns))
    kcols = slice(head * dk, (head + 1) * dk)
    vcols = slice(head * dv, (head + 1) * dv)

    def load(ref, cols):
        return jnp.concatenate([ref[0, row0 + ci * c:row0 + (ci + 1) * c, cols] for ci in order], axis=0)

    q = load(q_ref, kcols) * (dk ** -0.5)
    k = load(k_ref, kcols)
    z = jnp.dot(load(r_ref, slice(None)).astype(BF16), wup_ref[0, :, kcols], preferred_element_type=F32)
    g = jax.nn.log_sigmoid(z + bias_ref[0, :, kcols]) * (math.log2(math.e) / GATE_TAU)

    row = lax.broadcasted_iota(jnp.int32, (rows, rows), 0)
    col = lax.broadcasted_iota(jnp.int32, (rows, rows), 1)
    shift = c.bit_length() - 1
    row_chunk = lax.shift_right_logical(row, shift)
    col_chunk = lax.shift_right_logical(col, shift)
    causal = (col >= row) if reverse else (col <= row)
    same_chunk = (row_chunk == col_chunk) & causal
    tri = jnp.where(same_chunk, 1.0, 0.0).astype(BF16)
    g_hi = g.astype(BF16)
    g_lo = (g - g_hi.astype(F32)).astype(BF16)
    b = jnp.dot(tri, g_hi, preferred_element_type=F32) + jnp.dot(tri, g_lo, preferred_element_type=F32)

    last_i, mid_i = (0, c - c // 2) if reverse else (c - 1, c // 2 - 1)
    b_last = [b[p * c + last_i:p * c + last_i + 1, :] for p in range(ns)]
    b_mid = [b[p * c + mid_i:p * c + mid_i + 1, :] for p in range(ns)]
    a = [jnp.exp2(x) for x in b_last]
    yield False

    def prod(factors):
        out = None
        for f in factors:
            out = f if out is None else out * f
        return out

    q_intra, k_intra, q_inter, k_state, q_prev, k_next = [], [], [], [], [], []
    q_skip = {d: [] for d in range(2, ns)}
    for p in range(ns):
        rs = slice(p * c, (p + 1) * c)
        bp, qp, kp = b[rs], q[rs], k[rs]
        qn = qp * jnp.exp2(bp)
        q_intra.append((qp * jnp.exp2(bp - b_mid[p])).astype(BF16))
        q_inter.append(qn.astype(BF16))
        before = prod(a[:p])
        q_prev.append((qn if before is None else qn * before).astype(BF16))
        for d in range(2, p + 1):
            q_skip[d].append((qn * prod(a[p - d + 1:p])).astype(BF16))
        yield False
        ks = kp * jnp.exp2(b_last[p] - bp)
        k_intra.append((kp * jnp.exp2(b_mid[p] - bp)).astype(BF16))
        k_state.append(ks.astype(BF16))
        after = prod(a[p + 1:])
        k_next.append((ks if after is None else ks * after).astype(BF16))
        yield False
    v = load(v_ref, vcols).astype(BF16)
    yield True

    nt = (((1,), (1,)), ((), ()))
    s_intra = lax.dot_general(jnp.concatenate(q_intra, axis=0), jnp.concatenate(k_intra, axis=0), nt,
                              preferred_element_type=F32)
    yield None
    cross_lhs = q_inter + [x for d in range(2, ns) for x in q_skip[d]]
    s_cross = lax.dot_general(jnp.concatenate(cross_lhs, axis=0), jnp.concatenate(k_state, axis=0), nt,
                              preferred_element_type=F32)
    yield None

    skip_base = {}
    base = ns * c
    for d in range(2, ns):
        skip_base[d] = base
        base += (ns - d) * c
    score_rows = []
    row_in = lax.broadcasted_iota(jnp.int32, (c, rows), 0)
    col_in = lax.broadcasted_iota(jnp.int32, (c, rows), 1)
    col_chunk_in = lax.shift_right_logical(col_in, shift)
    for p in range(ns):
        rs = slice(p * c, (p + 1) * c)
        col_local = col_in - p * c
        causal_in = (col_local >= row_in) if reverse else (col_local <= row_in)
        sc = jnp.where((col_chunk_in == p) & causal_in, s_intra[rs], 0.0)
        if p >= 1:
            sc = jnp.where(col_chunk_in == p - 1, s_cross[rs], sc)
        for d in range(2, p + 1):
            off = skip_base[d] + (p - d) * c
            sc = jnp.where(col_chunk_in == p - d, s_cross[off:off + c], sc)
        score_rows.append(sc.astype(BF16))
    scores = jnp.concatenate(score_rows, axis=0)
    o = jnp.dot(scores, v, preferred_element_type=F32)
    yield None

    state = state_ref[head]
    o = o + jnp.dot(jnp.concatenate(q_prev, axis=0), state.astype(BF16), preferred_element_type=F32)
    yield None

    upd = lax.dot_general(jnp.concatenate(k_next, axis=0), v, (((0,), (0,)), ((), ())),
                          preferred_element_type=F32)
    decay_t = jnp.transpose(jnp.broadcast_to(prod(a), (LANES, dk)))
    state_ref[head] = _lane_tile(decay_t, dv) * state + upd
    yield None

    for p, ci in enumerate(order):
        finish(row0 + ci * c, o[p * c:(p + 1) * c])
        yield None


def _run_staggered(blocks):
    prev = None
    for gen in blocks:
        ready = False
        while not ready:
            ready = next(gen)
            if prev is not None:
                next(prev, None)
        if prev is not None:
            for _ in prev:
                pass
        prev = gen
    for _ in prev:
        pass


def _gla_kernel(*refs, nb, heads, sub, dk, dv, cast_gains):
    it = iter(refs)
    q_ref, k_ref, v_ref, r_ref, wup_ref, bias_ref, gate_ref, gnorm_ref = (next(it) for _ in range(8))
    cast_in = _take_cast_refs(it, cast_gains)
    o_ref = next(it)
    cast_out = [next(it) for _ in cast_gains]
    state_ref, acc_ref = next(it), next(it)

    s = pl.program_id(2)
    rows = q_ref.shape[1]
    stages = functools.partial(_gla_block_stages, q_ref, k_ref, v_ref, r_ref, wup_ref, bias_ref, state_ref,
                               rows=sub, dk=dk, dv=dv)
    starts = tuple(range(0, rows, sub))

    @pl.when((s == 0) | (s == nb))
    def _():
        state_ref[...] = jnp.zeros_like(state_ref)

    @pl.when(s < nb)
    def _():
        base = pl.multiple_of(s * rows, rows)

        def keep(head, row, piece):
            acc_ref[pl.ds(base + row, CHUNK), head * dv:(head + 1) * dv] = piece

        _run_staggered([stages(functools.partial(keep, head), head=head, row0=row0, reverse=False)
                        for head in range(heads) for row0 in starts])

    @pl.when(s >= nb)
    def _():
        base = pl.multiple_of((2 * nb - 1 - s) * rows, rows)
        gnorm = gnorm_ref[...]

        def emit(head, row, piece):
            vcols = slice(head * dv, (head + 1) * dv)
            o = _rmsnorm_rows(acc_ref[pl.ds(base + row, CHUNK), vcols] + piece, gnorm)
            gate = gate_ref[0, row:row + CHUNK, vcols]
            o_ref[0, row:row + CHUNK, vcols] = (o * jax.nn.silu(gate)).astype(o_ref.dtype)

        _run_staggered([stages(functools.partial(emit, head), head=head, row0=row0, reverse=True)
                        for head in range(heads) for row0 in reversed(starts)])

    _run_casts(cast_in, cast_out)


def gla_core(p, r, wup, bias, g_norm, *, heads, casts=(), bs=1024, sub=256, heads_per_step=1):
    b, s, width = p.shape
    dv_total = width // 3
    dk_total = dv_total // 2
    dk, dv = dk_total // heads, dv_total // heads
    hp = heads_per_step
    bs = min(bs, s)
    assert heads % hp == 0 and s % bs == 0 and bs % sub == 0 and sub % CHUNK == 0
    gk, gv = hp * dk, hp * dv
    nb = s // bs
    kq = dk_total // gk
    kv = (2 * dk_total) // gv
    kg = (2 * dk_total + dv_total) // gv
    rw = r.shape[-1]
    grid = (b, heads // hp, 2 * nb)

    def blk(i):
        return jnp.where(i < nb, i, 2 * nb - 1 - i)

    def out_blk(i):
        return jnp.where(i < nb, nb - 1, 2 * nb - 1 - i)

    in_specs = [
        pl.BlockSpec((1, bs, gk), lambda bi, h, i: (bi, blk(i), h)),
        pl.BlockSpec((1, bs, gk), lambda bi, h, i: (bi, blk(i), kq + h)),
        pl.BlockSpec((1, bs, gv), lambda bi, h, i: (bi, blk(i), kv + h)),
        pl.BlockSpec((1, bs, rw), lambda bi, h, i: (bi, blk(i), 0)),
        pl.BlockSpec((1, rw, gk), lambda bi, h, i: (i // nb, 0, h)),
        pl.BlockSpec((1, 1, gk), lambda bi, h, i: (i // nb, 0, h)),
        pl.BlockSpec((1, bs, gv), lambda bi, h, i: (bi, out_blk(i), kg + h)),
        pl.BlockSpec((1, dv), lambda bi, h, i: (0, 0)),
    ]
    args = [p, p, p, r, wup, bias, p, g_norm.reshape(1, dv)]
    out_specs = [pl.BlockSpec((1, bs, gv), lambda bi, h, i: (bi, out_blk(i), h))]
    out_shape = [jax.ShapeDtypeStruct((b, s, dv_total), BF16)]
    blocks = (2 * _nbytes((bs, gk), F32) + 2 * _nbytes((bs, gv), F32) + _nbytes((bs, rw), F32)
              + _nbytes((bs, gv), BF16))
    blocks += _add_casts(casts, grid, in_specs, args, out_specs, out_shape)
    scratch_bytes = _nbytes((hp, dk, dv), F32) + _nbytes((s, gv), F32)
    return pl.pallas_call(
        functools.partial(_gla_kernel, nb=nb, heads=hp, sub=sub, dk=dk, dv=dv,
                          cast_gains=tuple(job.gain is not None for job in casts)),
        grid=grid,
        in_specs=in_specs,
        out_specs=out_specs,
        out_shape=out_shape,
        scratch_shapes=[pltpu.VMEM((hp, dk, dv), F32), pltpu.VMEM((s, gv), F32)],
        compiler_params=pltpu.CompilerParams(
            dimension_semantics=("arbitrary", "arbitrary", "arbitrary"),
            vmem_limit_bytes=_vmem_limit(2 * blocks + scratch_bytes)),
        name="gla_scan",
    )(*args)


def gla_layer(h, hb, h_ss, w_in_t, w_out_f32, layer, w_up_f, b_up_f, w_up_b, b_up_b, g_norm, next_cast):
    b, s, d = h.shape
    t = b * s
    main = w_in_t.shape[0] - 2 * GATE_RANK
    h2 = h.reshape(t, d)
    w_r_t = jnp.pad(w_in_t[main:], ((0, LANES - 2 * GATE_RANK), (0, 0)))
    p, r, w_out = matmul(hb, w_in_t, F32, n=main, w_transposed=True, row_ss=h_ss, side_w=w_r_t,
                         casts=(CastJob(w_out_f32, layer),), name="gla_in_proj")
    wup = jnp.stack([jnp.pad(w_up_f, ((0, LANES - GATE_RANK), (0, 0))),
                     jnp.pad(w_up_b, ((GATE_RANK, LANES - 2 * GATE_RANK), (0, 0)))]).astype(BF16)
    bias = jnp.stack([b_up_f, b_up_b])[:, None, :]
    og, next_bf16 = gla_core(p.reshape(b, s, main), r.reshape(b, s, LANES), wup, bias, g_norm, heads=GLA_HEADS,
                             casts=(next_cast,))
    out, out_b, out_ss = matmul(og.reshape(t, -1), w_out, F32, residual=h2, stats=True, name="gla_out_proj")
    return out, out_b, out_ss, next_bf16


def kernel(x, norm_mix, norm_mlp, norm_final, pool_w, pool_scale, gla_w_in, gla_w_up_f, gla_b_up_f, gla_w_up_b,
           gla_b_up_b, gla_g_norm, gla_w_out, mlp_w_in, mlp_w_out):
    b, s, d = x.shape
    t = b * s
    w_in_t_f32 = jnp.swapaxes(gla_w_in, 1, 2)
    h, hn, w1 = pool_layer(x, norm_mix[0], norm_mlp[0], pool_w[0], pool_scale[0], casts=(CastJob(mlp_w_in, 0),))
    a, w2, w_in = matmul(hn.reshape(t, d), w1, BF16, relu2=True,
                         casts=(CastJob(mlp_w_out, 0), CastJob(w_in_t_f32, 0, gain=norm_mix[1], gain_axis=1)),
                         name="mlp_up_proj")
    h, hb, h_ss = matmul_ktiled(a, w2, h.reshape(t, d), stats=True, name="mlp_down_proj")

    h, hb, h_ss, w1 = gla_layer(h.reshape(b, s, d), hb, h_ss, w_in, gla_w_out, 0, gla_w_up_f[0], gla_b_up_f[0],
                                gla_w_up_b[0], gla_b_up_b[0], gla_g_norm[0],
                                CastJob(mlp_w_in, 1, gain=norm_mlp[1]))

    a, w2 = matmul(hb, w1, BF16, relu2=True, row_ss=h_ss, casts=(CastJob(mlp_w_out, 1),), name="mlp_up_proj")
    (h,) = matmul_ktiled(a, w2, h, name="mlp_down_proj")
    return rmsnorm(h, norm_final, F32).reshape(b, s, d)
```

```python
import functools
import math
from typing import NamedTuple, Optional

import jax
import jax.numpy as jnp
from jax import lax
from jax.experimental import pallas as pl
from jax.experimental.pallas import tpu as pltpu

EPS = 1e-6
POOL_WINDOWS = (2, 4, 8, 16)
GLA_HEADS = 4
GATE_RANK = 16
GATE_TAU = 16.0
CHUNK = 64

V7X_VMEM_BYTES = 64 * 1024 * 1024
LANES = 128
SUBLANES = 8
MXU_COLS = 256

F32 = jnp.float32
BF16 = jnp.bfloat16


def _vmem_limit(block_bytes):
    return int(min(V7X_VMEM_BYTES - 4 * 1024 * 1024, 2 * block_bytes))


def _nbytes(shape, dtype):
    n = 1
    for s in shape:
        n *= s
    return n * jnp.dtype(dtype).itemsize


def _lane_tile(x, n):
    return jnp.concatenate([x] * (n // LANES), axis=1)


def _row_sumsq(x):
    return jnp.broadcast_to(jnp.sum(x * x, axis=-1, keepdims=True), (x.shape[0], LANES))


def _rmsnorm_rows(x, g):
    y = x * lax.rsqrt(jnp.mean(x * x, axis=-1, keepdims=True) + EPS)
    return y * g


class CastJob(NamedTuple):
    src: jax.Array
    layer: int
    gain: Optional[jax.Array] = None
    gain_axis: int = 0


class _CastPlan(NamedTuple):
    in_specs: list
    args: list
    out_spec: pl.BlockSpec
    out_shape: jax.ShapeDtypeStruct
    nbytes: int


def _plan_cast(job, grid):
    _, r, c = job.src.shape
    steps = 1
    for g in grid:
        steps *= g
    bf16_rows = 2 * SUBLANES
    nslabs = next(n for n in range(min(steps, r // bf16_rows), 0, -1) if r % (n * bf16_rows) == 0)
    slab = r // nslabs

    def slab_of(*idx):
        step = 0
        for g, i in zip(grid, idx):
            step = step * g + i
        return step * nslabs // steps

    layer = job.layer
    in_specs = [pl.BlockSpec((None, slab, c), lambda *idx: (layer, slab_of(*idx), 0))]
    args = [job.src]
    if job.gain is not None and job.gain_axis == 0:
        in_specs.append(pl.BlockSpec((slab, 1), lambda *idx: (slab_of(*idx), 0)))
        args.append(job.gain.reshape(r, 1))
    elif job.gain is not None:
        in_specs.append(pl.BlockSpec((1, c), lambda *idx: (0, 0)))
        args.append(job.gain.reshape(1, c))
    return _CastPlan(in_specs, args, pl.BlockSpec((slab, c), lambda *idx: (slab_of(*idx), 0)),
                     jax.ShapeDtypeStruct((r, c), BF16), _nbytes((slab, c), F32) + _nbytes((slab, c), BF16))


def _take_cast_refs(it, gains):
    return [(next(it), next(it) if g else None) for g in gains]


def _run_casts(cast_in, cast_out, piece=0, pieces=1):
    for (src, gain), dst in zip(cast_in, cast_out):
        width = dst.shape[1] // pieces
        assert width % LANES == 0
        cols = slice(piece * width, (piece + 1) * width)
        val = src[:, cols]
        if gain is not None:
            val = val * (gain[:, cols] if gain.shape[1] == dst.shape[1] else gain[...])
        dst[:, cols] = val.astype(dst.dtype)


def _add_casts(casts, grid, in_specs, args, out_specs, out_shape):
    nbytes = 0
    for job in casts:
        plan = _plan_cast(job, grid)
        in_specs.extend(plan.in_specs)
        args.extend(plan.args)
        out_specs.append(plan.out_spec)
        out_shape.append(plan.out_shape)
        nbytes += plan.nbytes
    return nbytes


def _rmsnorm_kernel(x_ref, g_ref, o_ref):
    o_ref[...] = _rmsnorm_rows(x_ref[...], g_ref[...]).astype(o_ref.dtype)


def rmsnorm(x, g, out_dtype, tm=512):
    m, d = x.shape
    blocks = 2 * (_nbytes((tm, d), x.dtype) + _nbytes((tm, d), out_dtype))
    return pl.pallas_call(
        _rmsnorm_kernel,
        grid=(m // tm,),
        in_specs=[pl.BlockSpec((tm, d), lambda i: (i, 0)), pl.BlockSpec((1, d), lambda i: (0, 0))],
        out_specs=pl.BlockSpec((tm, d), lambda i: (i, 0)),
        out_shape=jax.ShapeDtypeStruct((m, d), out_dtype),
        compiler_params=pltpu.CompilerParams(
            dimension_semantics=("arbitrary",), vmem_limit_bytes=_vmem_limit(blocks)),
        name="rmsnorm",
    )(x, g.reshape(1, d))


def _accumulate_row_stats(ss_ref, part, first):
    @pl.when(first)
    def _():
        ss_ref[...] = part

    @pl.when(jnp.logical_not(first))
    def _():
        ss_ref[...] += part


def _mm_kernel(*refs, relu2, residual, scaled, stats, side, cast_gains, inv_d, w_dims, silu_from):
    it = iter(refs)
    x_ref, w_ref = next(it), next(it)
    res_ref = next(it) if residual else None
    ss_in_ref = next(it) if scaled else None
    side_w_ref = next(it) if side else None
    cast_in = _take_cast_refs(it, cast_gains)
    o_ref = next(it)
    ob_ref, ss_out_ref = (next(it), next(it)) if stats else (None, None)
    side_o_ref = next(it) if side else None
    cast_out = [next(it) for _ in cast_gains]

    j = pl.program_id(1)
    w_transposed = w_dims[0][1][0] == 1
    row_scale = lax.rsqrt(ss_in_ref[...] * inv_d + EPS) if scaled else None
    width = MXU_COLS
    tiles = o_ref.shape[1] // width

    def body(silu):
        ss_part = None
        for tile in range(tiles):
            cols = slice(tile * width, (tile + 1) * width)
            w_tile = w_ref[cols, :] if w_transposed else w_ref[:, cols]
            acc = lax.dot_general(x_ref[...], w_tile, w_dims, preferred_element_type=F32)
            if scaled:
                acc = acc * _lane_tile(row_scale, width)
            if relu2:
                acc = jnp.maximum(acc, 0.0)
                acc = acc * acc
            if silu:
                acc = jax.nn.silu(acc)
            if residual:
                acc = res_ref[:, cols] + acc
            o_ref[:, cols] = acc.astype(o_ref.dtype)
            if stats:
                ob_ref[:, cols] = acc.astype(ob_ref.dtype)
                part = _row_sumsq(acc)
                ss_part = part if ss_part is None else ss_part + part
            _run_casts(cast_in, cast_out, tile, tiles)
        if stats:
            _accumulate_row_stats(ss_out_ref, ss_part, j == 0)

    if silu_from is None:
        body(False)
    else:
        in_silu = j * o_ref.shape[1] >= silu_from
        pl.when(in_silu)(functools.partial(body, True))
        pl.when(jnp.logical_not(in_silu))(functools.partial(body, False))
    if side:
        @pl.when(j == 0)
        def _():
            extra = lax.dot_general(x_ref[...], side_w_ref[...], w_dims, preferred_element_type=F32)
            side_o_ref[...] = extra * row_scale if scaled else extra


def matmul(x, w, out_dtype, *, n=None, w_transposed=False, relu2=False, silu_from=None, residual=None, row_ss=None,
           stats=False, side_w=None, casts=(), tm=1024, tn=1024, name="matmul"):
    m, k = x.shape
    n = w.shape[0 if w_transposed else 1] if n is None else n
    tn = min(tn, n)
    assert m % tm == 0 and n % tn == 0 and (silu_from is None or silu_from % tn == 0)
    grid = (m // tm, n // tn)
    row_block = pl.BlockSpec((tm, LANES), lambda i, j: (i, 0))
    tile = pl.BlockSpec((tm, tn), lambda i, j: (i, j))
    if w_transposed:
        w_spec, side_spec = pl.BlockSpec((tn, k), lambda i, j: (j, 0)), pl.BlockSpec((LANES, k), lambda i, j: (0, 0))
    else:
        w_spec, side_spec = pl.BlockSpec((k, tn), lambda i, j: (0, j)), pl.BlockSpec((k, LANES), lambda i, j: (0, 0))
    in_specs = [pl.BlockSpec((tm, k), lambda i, j: (i, 0)), w_spec]
    args = [x, w]
    blocks = _nbytes((tm, k), x.dtype) + _nbytes((k, tn), w.dtype) + _nbytes((tm, tn), out_dtype)
    if residual is not None:
        in_specs.append(tile)
        args.append(residual)
        blocks += _nbytes((tm, tn), residual.dtype)
    if row_ss is not None:
        in_specs.append(row_block)
        args.append(row_ss)
    if side_w is not None:
        in_specs.append(side_spec)
        args.append(side_w)
        blocks += _nbytes((k, LANES), side_w.dtype)
    out_specs = [tile]
    out_shape = [jax.ShapeDtypeStruct((m, n), out_dtype)]
    if stats:
        out_specs += [tile, row_block]
        out_shape += [jax.ShapeDtypeStruct((m, n), BF16), jax.ShapeDtypeStruct((m, LANES), F32)]
        blocks += _nbytes((tm, tn), BF16)
    if side_w is not None:
        out_specs.append(row_block)
        out_shape.append(jax.ShapeDtypeStruct((m, LANES), F32))
    blocks += _add_casts(casts, grid, in_specs, args, out_specs, out_shape)
    return pl.pallas_call(
        functools.partial(_mm_kernel, relu2=relu2, residual=residual is not None, scaled=row_ss is not None,
                          stats=stats, side=side_w is not None,
                          cast_gains=tuple(job.gain is not None for job in casts), inv_d=1.0 / k,
                          w_dims=(((1,), (1 if w_transposed else 0,)), ((), ())), silu_from=silu_from),
        grid=grid,
        in_specs=in_specs,
        out_specs=out_specs,
        out_shape=out_shape,
        compiler_params=pltpu.CompilerParams(
            dimension_semantics=("arbitrary", "arbitrary"),
            vmem_limit_bytes=_vmem_limit(2 * blocks + _nbytes((tm, tn), F32))),
        name=name,
    )(*args)


def _mm_ktiled_kernel(x_ref, w_ref, res_ref, o_ref, *stat_refs, nk):
    j, kk = pl.program_id(1), pl.program_id(2)
    col_tiles = [slice(c0, c0 + MXU_COLS) for c0 in range(0, o_ref.shape[1], MXU_COLS)]

    def step(first, final):
        ss_part = None
        for cols in col_tiles:
            base = res_ref[:, cols] if first else o_ref[:, cols]
            out = base + jnp.dot(x_ref[...], w_ref[:, cols], preferred_element_type=F32)
            o_ref[:, cols] = out
            if final and stat_refs:
                stat_refs[0][:, cols] = out.astype(stat_refs[0].dtype)
                part = _row_sumsq(out)
                ss_part = part if ss_part is None else ss_part + part
        if final and stat_refs:
            _accumulate_row_stats(stat_refs[1], ss_part, j == 0)

    if nk == 1:
        step(True, True)
        return
    pl.when(kk == 0)(functools.partial(step, True, False))
    if stat_refs:
        if nk > 2:
            pl.when((kk > 0) & (kk < nk - 1))(functools.partial(step, False, False))
        pl.when(kk == nk - 1)(functools.partial(step, False, True))
    else:
        pl.when(kk > 0)(functools.partial(step, False, False))


def matmul_ktiled(x, w, residual, *, stats=False, tm=1024, tn=1024, tk=4096, name="matmul_ktiled"):
    m, k = x.shape
    n = w.shape[1]
    assert m % tm == 0 and n % tn == 0 and k % tk == 0
    tile = pl.BlockSpec((tm, tn), lambda i, j, kk: (i, j))
    out_specs = [tile]
    out_shape = [jax.ShapeDtypeStruct((m, n), F32)]
    blocks = (_nbytes((tm, tk), x.dtype) + _nbytes((tk, tn), w.dtype) + 2 * _nbytes((tm, tn), F32))
    if stats:
        out_specs += [tile, pl.BlockSpec((tm, LANES), lambda i, j, kk: (i, 0))]
        out_shape += [jax.ShapeDtypeStruct((m, n), BF16), jax.ShapeDtypeStruct((m, LANES), F32)]
        blocks += _nbytes((tm, tn), BF16)
    return pl.pallas_call(
        functools.partial(_mm_ktiled_kernel, nk=k // tk),
        grid=(m // tm, n // tn, k // tk),
        in_specs=[
            pl.BlockSpec((tm, tk), lambda i, j, kk: (i, kk)),
            pl.BlockSpec((tk, tn), lambda i, j, kk: (kk, j)),
            tile,
        ],
        out_specs=out_specs,
        out_shape=out_shape,
        compiler_params=pltpu.CompilerParams(
            dimension_semantics=("arbitrary", "arbitrary", "arbitrary"),
            vmem_limit_bytes=_vmem_limit(2 * blocks + _nbytes((tm, tn), F32))),
        name=name,
    )(x, w, residual)


def _pool_kernel(*refs, tm, seq, halo, cast_gains):
    it = iter(refs)
    x_ref, xp_ref, xn_ref, gmix_ref, gmlp_ref, w_ref, scale_ref = (next(it) for _ in range(7))
    cast_in = _take_cast_refs(it, cast_gains)
    h_ref, hn_ref = next(it), next(it)
    cast_out = [next(it) for _ in cast_gains]
    buf_ref, lvl_ref = next(it), next(it)

    i = pl.program_id(1)
    nblk = pl.num_programs(1)
    gmix = gmix_ref[...]
    x = x_ref[0]
    d_model = x.shape[-1]
    group = d_model // len(POOL_WINDOWS)

    ext = tm + 2 * halo
    buf_ref[halo:halo + tm, :] = _rmsnorm_rows(x, gmix)
    buf_ref[0:halo, :] = jnp.where(i > 0, _rmsnorm_rows(xp_ref[0], gmix), 0.0)
    buf_ref[halo + tm:ext, :] = jnp.where(i < nblk - 1, _rmsnorm_rows(xn_ref[0], gmix), 0.0)
    buf_ref[ext:ext + halo, :] = jnp.zeros((halo, d_model), F32)
    lvl_ref[:, ext:ext + halo, :] = jnp.zeros((2, halo, group), F32)

    pos = i * tm + lax.broadcasted_iota(jnp.int32, (tm, 1), 0)
    for gi, win in enumerate(POOL_WINDOWS):
        cols = slice(gi * group, (gi + 1) * group)
        half = win // 2
        load = lambda lo, n: buf_ref[lo:lo + n, cols]
        span, slot = 1, 0
        while span < half:
            lvl_ref[slot, 0:ext, :] = load(0, ext) + load(span, ext)
            load = functools.partial(lambda s, lo, n: lvl_ref[s, lo:lo + n, :], slot)
            span, slot = 2 * span, 1 - slot
        acc = load(halo - half, tm) + load(halo, tm)
        count = (jnp.minimum(pos + half, seq) - jnp.maximum(pos - half, 0)).astype(F32)
        diff = acc * (1.0 / count) - buf_ref[halo:halo + tm, cols]
        y = jnp.dot(diff.astype(BF16), w_ref[gi], preferred_element_type=F32)
        h_ref[0, :, cols] = x[:, cols] + y * scale_ref[:, cols]

    hn_ref[0] = _rmsnorm_rows(h_ref[0], gmlp_ref[...]).astype(hn_ref.dtype)
    _run_casts(cast_in, cast_out)


def pool_layer(x, g_mix, g_mlp, pool_w, pool_scale, casts=(), tm=256):
    b, s, d = x.shape
    halo = SUBLANES
    assert max(POOL_WINDOWS) // 2 <= halo and tm % halo == 0 and s % tm == 0
    ng, gd, _ = pool_w.shape
    hb = tm // halo
    grid = (b, s // tm)
    row_block = pl.BlockSpec((1, tm, d), lambda bi, i: (bi, i, 0))
    vec = pl.BlockSpec((1, d), lambda bi, i: (0, 0))
    in_specs = [
        row_block,
        pl.BlockSpec((1, halo, d), lambda bi, i: (bi, jnp.maximum(i * hb - 1, 0), 0)),
        pl.BlockSpec((1, halo, d), lambda bi, i: (bi, jnp.minimum((i + 1) * hb, s // halo - 1), 0)),
        vec,
        vec,
        pl.BlockSpec((ng, gd, gd), lambda bi, i: (0, 0, 0), pipeline_mode=pl.Buffered(1)),
        vec,
    ]
    args = [x, x, x, g_mix.reshape(1, d), g_mlp.reshape(1, d), pool_w.astype(BF16), pool_scale.reshape(1, d)]
    out_specs = [row_block, row_block]
    out_shape = [jax.ShapeDtypeStruct((b, s, d), F32), jax.ShapeDtypeStruct((b, s, d), BF16)]
    blocks = (2 * _nbytes((tm, d), F32) + _nbytes((tm, d), BF16)) * 2 + _nbytes(pool_w.shape, BF16)
    blocks += 2 * _add_casts(casts, grid, in_specs, args, out_specs, out_shape)
    scratch = [pltpu.VMEM((tm + 3 * halo, d), F32), pltpu.VMEM((2, tm + 3 * halo, gd), F32)]
    scratch_bytes = _nbytes((tm + 3 * halo, d + 2 * gd), F32)
    return pl.pallas_call(
        functools.partial(_pool_kernel, tm=tm, seq=s, halo=halo,
                          cast_gains=tuple(job.gain is not None for job in casts)),
        grid=grid,
        in_specs=in_specs,
        out_specs=out_specs,
        out_shape=out_shape,
        scratch_shapes=scratch,
        compiler_params=pltpu.CompilerParams(
            dimension_semantics=("arbitrary", "arbitrary"),
            vmem_limit_bytes=_vmem_limit(blocks + scratch_bytes)),
        name="pool_layer",
    )(*args)


def _gla_block_stages(q_ref, k_ref, v_ref, r_ref, wup_ref, bias_ref, state_ref, finish, *, head, row0, rows, dk, dv,
                       reverse):
    c = CHUNK
    ns = rows // c
    order = tuple(reversed(range(ns))) if reverse else tuple(range(ns))
    kcols = slice(head * dk, (head + 1) * dk)
    vcols = slice(head * dv, (head + 1) * dv)

    def load(ref, cols):
        return jnp.concatenate([ref[0, row0 + ci * c:row0 + (ci + 1) * c, cols] for ci in order], axis=0)

    q = load(q_ref, kcols) * (dk ** -0.5)
    k = load(k_ref, kcols)
    z = jnp.dot(load(r_ref, slice(None)).astype(BF16), wup_ref[0, :, kcols], preferred_element_type=F32)
    g = jax.nn.log_sigmoid(z + bias_ref[0, :, kcols]) * (math.log2(math.e) / GATE_TAU)

    row = lax.broadcasted_iota(jnp.int32, (rows, rows), 0)
    col = lax.broadcasted_iota(jnp.int32, (rows, rows), 1)
    shift = c.bit_length() - 1
    row_chunk = lax.shift_right_logical(row, shift)
    col_chunk = lax.shift_right_logical(col, shift)
    causal = (col >= row) if reverse else (col <= row)
    same_chunk = (row_chunk == col_chunk) & causal
    tri = jnp.where(same_chunk, 1.0, 0.0).astype(BF16)
    g_hi = g.astype(BF16)
    g_lo = (g - g_hi.astype(F32)).astype(BF16)
    b = jnp.dot(tri, g_hi, preferred_element_type=F32) + jnp.dot(tri, g_lo, preferred_element_type=F32)

    last_i, mid_i = (0, c - c // 2) if reverse else (c - 1, c // 2 - 1)
    b_last = [b[p * c + last_i:p * c + last_i + 1, :] for p in range(ns)]
    b_mid = [b[p * c + mid_i:p * c + mid_i + 1, :] for p in range(ns)]
    a = [jnp.exp2(x) for x in b_last]
    yield False

    def prod(factors):
        out = None
        for f in factors:
            out = f if out is None else out * f
        return out

    q_intra, k_intra, q_inter, k_state, q_prev, k_next = [], [], [], [], [], []
    q_skip = {d: [] for d in range(2, ns)}
    for p in range(ns):
        rs = slice(p * c, (p + 1) * c)
        bp, qp, kp = b[rs], q[rs], k[rs]
        qn = qp * jnp.exp2(bp)
        q_intra.append((qp * jnp.exp2(bp - b_mid[p])).astype(BF16))
        q_inter.append(qn.astype(BF16))
        before = prod(a[:p])
        q_prev.append((qn if before is None else qn * before).astype(BF16))
        for d in range(2, p + 1):
            q_skip[d].append((qn * prod(a[p - d + 1:p])).astype(BF16))
        yield False
        ks = kp * jnp.exp2(b_last[p] - bp)
        k_intra.append((kp * jnp.exp2(b_mid[p] - bp)).astype(BF16))
        k_state.append(ks.astype(BF16))
        after = prod(a[p + 1:])
        k_next.append((ks if after is None else ks * after).astype(BF16))
        yield False
    v = load(v_ref, vcols).astype(BF16)
    yield True

    nt = (((1,), (1,)), ((), ()))
    s_intra = lax.dot_general(jnp.concatenate(q_intra, axis=0), jnp.concatenate(k_intra, axis=0), nt,
                              preferred_element_type=F32)
    yield None
    cross_lhs = q_inter + [x for d in range(2, ns) for x in q_skip[d]]
    s_cross = lax.dot_general(jnp.concatenate(cross_lhs, axis=0), jnp.concatenate(k_state, axis=0), nt,
                              preferred_element_type=F32)
    yield None

    skip_base = {}
    base = ns * c
    for d in range(2, ns):
        skip_base[d] = base
        base += (ns - d) * c
    score_rows = []
    row_in = lax.broadcasted_iota(jnp.int32, (c, rows), 0)
    col_in = lax.broadcasted_iota(jnp.int32, (c, rows), 1)
    col_chunk_in = lax.shift_right_logical(col_in, shift)
    for p in range(ns):
        rs = slice(p * c, (p + 1) * c)
        col_local = col_in - p * c
        causal_in = (col_local >= row_in) if reverse else (col_local <= row_in)
        sc = jnp.where((col_chunk_in == p) & causal_in, s_intra[rs], 0.0)
        if p >= 1:
            sc = jnp.where(col_chunk_in == p - 1, s_cross[rs], sc)
        for d in range(2, p + 1):
            off = skip_base[d] + (p - d) * c
            sc = jnp.where(col_chunk_in == p - d, s_cross[off:off + c], sc)
        score_rows.append(sc.astype(BF16))
    scores = jnp.concatenate(score_rows, axis=0)
    o = jnp.dot(scores, v, preferred_element_type=F32)
    yield None

    state = state_ref[head]
    o = o + jnp.dot(jnp.concatenate(q_prev, axis=0), state.astype(BF16), preferred_element_type=F32)
    yield None

    upd = lax.dot_general(jnp.concatenate(k_next, axis=0), v, (((0,), (0,)), ((), ())),
                          preferred_element_type=F32)
    decay_t = jnp.transpose(jnp.broadcast_to(prod(a), (LANES, dk)))
    state_ref[head] = _lane_tile(decay_t, dv) * state + upd
    yield None

    for p, ci in enumerate(order):
        finish(row0 + ci * c, o[p * c:(p + 1) * c])
        yield None


def _run_staggered(blocks):
    prev = None
    for gen in blocks:
        ready = False
        while not ready:
            ready = next(gen)
            if prev is not None:
                next(prev, None)
        if prev is not None:
            for _ in prev:
                pass
        prev = gen
    for _ in prev:
        pass


def _gla_kernel(*refs, nb, heads, sub, dk, dv, cast_gains):
    it = iter(refs)
    q_ref, k_ref, v_ref, r_ref, wup_ref, bias_ref, gate_ref = (next(it) for _ in range(7))
    cast_in = _take_cast_refs(it, cast_gains)
    o_ref = next(it)
    cast_out = [next(it) for _ in cast_gains]
    state_ref, acc_ref = next(it), next(it)

    s = pl.program_id(2)
    rows = q_ref.shape[1]
    stages = functools.partial(_gla_block_stages, q_ref, k_ref, v_ref, r_ref, wup_ref, bias_ref, state_ref,
                               rows=sub, dk=dk, dv=dv)
    starts = tuple(range(0, rows, sub))

    @pl.when((s == 0) | (s == nb))
    def _():
        state_ref[...] = jnp.zeros_like(state_ref)

    @pl.when(s < nb)
    def _():
        base = pl.multiple_of(s * rows, rows)

        def keep(head, row, piece):
            acc_ref[pl.ds(base + row, CHUNK), head * dv:(head + 1) * dv] = piece

        _run_casts(cast_in, cast_out)
        _run_staggered([stages(functools.partial(keep, head), head=head, row0=row0, reverse=False)
                        for head in range(heads) for row0 in starts])

    @pl.when(s >= nb)
    def _():
        base = pl.multiple_of((2 * nb - 1 - s) * rows, rows)

        def emit(head, row, piece):
            vcols = slice(head * dv, (head + 1) * dv)
            o = acc_ref[pl.ds(base + row, CHUNK), vcols] + piece
            o = o * lax.rsqrt(jnp.mean(o * o, axis=-1, keepdims=True) + EPS)
            o_ref[0, row:row + CHUNK, vcols] = (o * gate_ref[0, row:row + CHUNK, vcols]).astype(o_ref.dtype)

        _run_casts(cast_in, cast_out)
        _run_staggered([stages(functools.partial(emit, head), head=head, row0=row0, reverse=True)
                        for head in range(heads) for row0 in reversed(starts)])


def gla_core(p, r, wup, bias, *, heads, casts=(), bs=1024, sub=256, heads_per_step=1):
    b, s, width = p.shape
    dv_total = width // 3
    dk_total = dv_total // 2
    dk, dv = dk_total // heads, dv_total // heads
    hp = heads_per_step
    bs = min(bs, s)
    assert heads % hp == 0 and s % bs == 0 and bs % sub == 0 and sub % CHUNK == 0
    gk, gv = hp * dk, hp * dv
    nb = s // bs
    kq = dk_total // gk
    kv = (2 * dk_total) // gv
    kg = (2 * dk_total + dv_total) // gv
    rw = r.shape[-1]
    grid = (b, heads // hp, 2 * nb)

    def blk(i):
        return jnp.where(i < nb, i, 2 * nb - 1 - i)

    def out_blk(i):
        return jnp.where(i < nb, nb - 1, 2 * nb - 1 - i)

    in_specs = [
        pl.BlockSpec((1, bs, gk), lambda bi, h, i: (bi, blk(i), h)),
        pl.BlockSpec((1, bs, gk), lambda bi, h, i: (bi, blk(i), kq + h)),
        pl.BlockSpec((1, bs, gv), lambda bi, h, i: (bi, blk(i), kv + h)),
        pl.BlockSpec((1, bs, rw), lambda bi, h, i: (bi, blk(i), 0)),
        pl.BlockSpec((1, rw, gk), lambda bi, h, i: (i // nb, 0, h)),
        pl.BlockSpec((1, 1, gk), lambda bi, h, i: (i // nb, 0, h)),
        pl.BlockSpec((1, bs, gv), lambda bi, h, i: (bi, out_blk(i), kg + h)),
    ]
    args = [p, p, p, r, wup, bias, p]
    out_specs = [pl.BlockSpec((1, bs, gv), lambda bi, h, i: (bi, out_blk(i), h))]
    out_shape = [jax.ShapeDtypeStruct((b, s, dv_total), BF16)]
    blocks = (2 * _nbytes((bs, gk), F32) + 2 * _nbytes((bs, gv), F32) + _nbytes((bs, rw), F32)
              + _nbytes((bs, gv), BF16))
    blocks += _add_casts(casts, grid, in_specs, args, out_specs, out_shape)
    scratch_bytes = _nbytes((hp, dk, dv), F32) + _nbytes((s, gv), F32)
    return pl.pallas_call(
        functools.partial(_gla_kernel, nb=nb, heads=hp, sub=sub, dk=dk, dv=dv,
                          cast_gains=tuple(job.gain is not None for job in casts)),
        grid=grid,
        in_specs=in_specs,
        out_specs=out_specs,
        out_shape=out_shape,
        scratch_shapes=[pltpu.VMEM((hp, dk, dv), F32), pltpu.VMEM((s, gv), F32)],
        compiler_params=pltpu.CompilerParams(
            dimension_semantics=("arbitrary", "arbitrary", "arbitrary"),
            vmem_limit_bytes=_vmem_limit(2 * blocks + scratch_bytes)),
        name="gla_scan",
    )(*args)


def gla_layer(h, hb, h_ss, w_in_t, w_out_f32, layer, w_up_f, b_up_f, w_up_b, b_up_b, g_norm, next_cast):
    b, s, d = h.shape
    t = b * s
    main = w_in_t.shape[0] - 2 * GATE_RANK
    h2 = h.reshape(t, d)
    w_r_t = jnp.pad(w_in_t[main:], ((0, LANES - 2 * GATE_RANK), (0, 0)))
    p, r, w_out = matmul(hb, w_in_t, F32, n=main, w_transposed=True, silu_from=main - w_out_f32.shape[1],
                         row_ss=h_ss, side_w=w_r_t,
                         casts=(CastJob(w_out_f32, layer, gain=jnp.tile(g_norm, GLA_HEADS)),), name="gla_in_proj")
    wup = jnp.stack([jnp.pad(w_up_f, ((0, LANES - GATE_RANK), (0, 0))),
                     jnp.pad(w_up_b, ((GATE_RANK, LANES - 2 * GATE_RANK), (0, 0)))]).astype(BF16)
    bias = jnp.stack([b_up_f, b_up_b])[:, None, :]
    og, next_bf16 = gla_core(p.reshape(b, s, main), r.reshape(b, s, LANES), wup, bias, heads=GLA_HEADS,
                             casts=(next_cast,))
    out, out_b, out_ss = matmul(og.reshape(t, -1), w_out, F32, residual=h2, stats=True, name="gla_out_proj")
    return out, out_b, out_ss, next_bf16


def kernel(x, norm_mix, norm_mlp, norm_final, pool_w, pool_scale, gla_w_in, gla_w_up_f, gla_b_up_f, gla_w_up_b,
           gla_b_up_b, gla_g_norm, gla_w_out, mlp_w_in, mlp_w_out):
    b, s, d = x.shape
    t = b * s
    w_in_t_f32 = jnp.swapaxes(gla_w_in, 1, 2)
    h, hn, w1 = pool_layer(x, norm_mix[0], norm_mlp[0], pool_w[0], pool_scale[0], casts=(CastJob(mlp_w_in, 0),))
    a, w2, w_in = matmul(hn.reshape(t, d), w1, BF16, relu2=True,
                         casts=(CastJob(mlp_w_out, 0), CastJob(w_in_t_f32, 0, gain=norm_mix[1], gain_axis=1)),
                         name="mlp_up_proj")
    h, hb, h_ss = matmul_ktiled(a, w2, h.reshape(t, d), stats=True, name="mlp_down_proj")

    h, hb, h_ss, w1 = gla_layer(h.reshape(b, s, d), hb, h_ss, w_in, gla_w_out, 0, gla_w_up_f[0], gla_b_up_f[0],
                                gla_w_up_b[0], gla_b_up_b[0], gla_g_norm[0],
                                CastJob(mlp_w_in, 1, gain=norm_mlp[1]))

    a, w2 = matmul(hb, w1, BF16, relu2=True, row_ss=h_ss, casts=(CastJob(mlp_w_out, 1),), name="mlp_up_proj")
    (h,) = matmul_ktiled(a, w2, h, name="mlp_down_proj")
    return rmsnorm(h, norm_final, F32).reshape(b, s, d)
```

```python
import functools
import math
from typing import NamedTuple, Optional

import jax
import jax.numpy as jnp
from jax import lax
from jax.experimental import pallas as pl
from jax.experimental.pallas import tpu as pltpu

EPS = 1e-6
POOL_WINDOWS = (2, 4, 8, 16)
GLA_HEADS = 4
GATE_RANK = 16
GATE_TAU = 16.0
CHUNK = 64

V7X_VMEM_BYTES = 64 * 1024 * 1024
LANES = 128
SUBLANES = 8
MXU_COLS = 256

F32 = jnp.float32
BF16 = jnp.bfloat16


def _vmem_limit(block_bytes):
    return int(min(V7X_VMEM_BYTES - 4 * 1024 * 1024, 2 * block_bytes))


def _nbytes(shape, dtype):
    n = 1
    for s in shape:
        n *= s
    return n * jnp.dtype(dtype).itemsize


def _lane_tile(x, n):
    return jnp.concatenate([x] * (n // LANES), axis=1)


def _row_sumsq(x):
    return jnp.broadcast_to(jnp.sum(x * x, axis=-1, keepdims=True), (x.shape[0], LANES))


def _rmsnorm_rows(x, g):
    y = x * lax.rsqrt(jnp.mean(x * x, axis=-1, keepdims=True) + EPS)
    return y * g


class CastJob(NamedTuple):
    src: jax.Array
    layer: int
    gain: Optional[jax.Array] = None
    gain_axis: int = 0


class _CastPlan(NamedTuple):
    in_specs: list
    args: list
    out_spec: pl.BlockSpec
    out_shape: jax.ShapeDtypeStruct
    nbytes: int


def _plan_cast(job, grid):
    _, r, c = job.src.shape
    steps = 1
    for g in grid:
        steps *= g
    bf16_rows = 2 * SUBLANES
    nslabs = next(n for n in range(min(steps, r // bf16_rows), 0, -1) if r % (n * bf16_rows) == 0)
    slab = r // nslabs

    def slab_of(*idx):
        step = 0
        for g, i in zip(grid, idx):
            step = step * g + i
        return step * nslabs // steps

    layer = job.layer
    in_specs = [pl.BlockSpec((None, slab, c), lambda *idx: (layer, slab_of(*idx), 0))]
    args = [job.src]
    if job.gain is not None and job.gain_axis == 0:
        in_specs.append(pl.BlockSpec((slab, 1), lambda *idx: (slab_of(*idx), 0)))
        args.append(job.gain.reshape(r, 1))
    elif job.gain is not None:
        in_specs.append(pl.BlockSpec((1, c), lambda *idx: (0, 0)))
        args.append(job.gain.reshape(1, c))
    return _CastPlan(in_specs, args, pl.BlockSpec((slab, c), lambda *idx: (slab_of(*idx), 0)),
                     jax.ShapeDtypeStruct((r, c), BF16), _nbytes((slab, c), F32) + _nbytes((slab, c), BF16))


def _take_cast_refs(it, gains):
    return [(next(it), next(it) if g else None) for g in gains]


def _run_casts(cast_in, cast_out, piece=0, pieces=1):
    for (src, gain), dst in zip(cast_in, cast_out):
        width = dst.shape[1] // pieces
        assert width % LANES == 0
        cols = slice(piece * width, (piece + 1) * width)
        val = src[:, cols]
        if gain is not None:
            val = val * (gain[:, cols] if gain.shape[1] == dst.shape[1] else gain[...])
        dst[:, cols] = val.astype(dst.dtype)


def _add_casts(casts, grid, in_specs, args, out_specs, out_shape):
    nbytes = 0
    for job in casts:
        plan = _plan_cast(job, grid)
        in_specs.extend(plan.in_specs)
        args.extend(plan.args)
        out_specs.append(plan.out_spec)
        out_shape.append(plan.out_shape)
        nbytes += plan.nbytes
    return nbytes


def _rmsnorm_kernel(x_ref, g_ref, o_ref):
    o_ref[...] = _rmsnorm_rows(x_ref[...], g_ref[...]).astype(o_ref.dtype)


def rmsnorm(x, g, out_dtype, tm=512):
    m, d = x.shape
    blocks = 2 * (_nbytes((tm, d), x.dtype) + _nbytes((tm, d), out_dtype))
    return pl.pallas_call(
        _rmsnorm_kernel,
        grid=(m // tm,),
        in_specs=[pl.BlockSpec((tm, d), lambda i: (i, 0)), pl.BlockSpec((1, d), lambda i: (0, 0))],
        out_specs=pl.BlockSpec((tm, d), lambda i: (i, 0)),
        out_shape=jax.ShapeDtypeStruct((m, d), out_dtype),
        compiler_params=pltpu.CompilerParams(
            dimension_semantics=("arbitrary",), vmem_limit_bytes=_vmem_limit(blocks)),
        name="rmsnorm",
    )(x, g.reshape(1, d))


def _accumulate_row_stats(ss_ref, part, first):
    @pl.when(first)
    def _():
        ss_ref[...] = part

    @pl.when(jnp.logical_not(first))
    def _():
        ss_ref[...] += part


def _mm_kernel(*refs, relu2, residual, scaled, stats, side, gate_scale, cast_gains, inv_d, w_dims, silu_from):
    gates = gate_scale is not None
    it = iter(refs)
    x_ref, w_ref = next(it), next(it)
    res_ref = next(it) if residual else None
    ss_in_ref = next(it) if scaled else None
    side_w_ref = next(it) if side else None
    gate_w_ref, gate_b_ref = (next(it), next(it)) if gates else (None, None)
    cast_in = _take_cast_refs(it, cast_gains)
    o_ref = next(it)
    ob_ref, ss_out_ref = (next(it), next(it)) if stats else (None, None)
    side_o_ref = next(it) if side else None
    ghi_ref, glo_ref = (next(it), next(it)) if gates else (None, None)
    cast_out = [next(it) for _ in cast_gains]

    j = pl.program_id(1)
    w_transposed = w_dims[0][1][0] == 1
    row_scale = lax.rsqrt(ss_in_ref[...] * inv_d + EPS) if scaled else None
    width = MXU_COLS
    tiles = o_ref.shape[1] // width

    if side:
        @pl.when(j == 0)
        def _():
            extra = lax.dot_general(x_ref[...], side_w_ref[...], w_dims, preferred_element_type=F32)
            side_o_ref[...] = extra * row_scale if scaled else extra

    def gate_block(piece, pieces):
        n_rows = side_o_ref.shape[0] // pieces
        rows = slice(piece * n_rows, (piece + 1) * n_rows)
        z = jnp.dot(side_o_ref[rows, :].astype(BF16), gate_w_ref[...], preferred_element_type=F32) + gate_b_ref[...]
        z2 = z * math.log2(math.e)
        g = (jnp.minimum(z2, 0.0) - jnp.log2(1.0 + jnp.exp2(-jnp.abs(z2)))) * gate_scale
        hi = g.astype(BF16)
        ghi_ref[rows, :] = hi
        glo_ref[rows, :] = (g - hi.astype(F32)).astype(BF16)

    def body(silu):
        ss_part = None
        for tile in range(tiles):
            cols = slice(tile * width, (tile + 1) * width)
            w_tile = w_ref[cols, :] if w_transposed else w_ref[:, cols]
            acc = lax.dot_general(x_ref[...], w_tile, w_dims, preferred_element_type=F32)
            if scaled:
                acc = acc * _lane_tile(row_scale, width)
            if relu2:
                acc = jnp.maximum(acc, 0.0)
                acc = acc * acc
            if silu:
                acc = jax.nn.silu(acc)
            if residual:
                acc = res_ref[:, cols] + acc
            o_ref[:, cols] = acc.astype(o_ref.dtype)
            if stats:
                ob_ref[:, cols] = acc.astype(ob_ref.dtype)
                part = _row_sumsq(acc)
                ss_part = part if ss_part is None else ss_part + part
            _run_casts(cast_in, cast_out, tile, tiles)
            if gates and not silu:
                gate_block(tile, tiles)
        if stats:
            _accumulate_row_stats(ss_out_ref, ss_part, j == 0)

    if silu_from is None:
        body(False)
    else:
        in_silu = j * o_ref.shape[1] >= silu_from
        pl.when(in_silu)(functools.partial(body, True))
        pl.when(jnp.logical_not(in_silu))(functools.partial(body, False))


def matmul(x, w, out_dtype, *, n=None, w_transposed=False, relu2=False, silu_from=None, residual=None, row_ss=None,
           stats=False, side_w=None, gates=None, casts=(), tm=1024, tn=1024, name="matmul"):
    m, k = x.shape
    n = w.shape[0 if w_transposed else 1] if n is None else n
    tn = min(tn, n)
    assert m % tm == 0 and n % tn == 0 and (silu_from is None or silu_from % tn == 0)
    grid = (m // tm, n // tn)
    row_block = pl.BlockSpec((tm, LANES), lambda i, j: (i, 0))
    tile = pl.BlockSpec((tm, tn), lambda i, j: (i, j))
    if w_transposed:
        w_spec, side_spec = pl.BlockSpec((tn, k), lambda i, j: (j, 0)), pl.BlockSpec((LANES, k), lambda i, j: (0, 0))
    else:
        w_spec, side_spec = pl.BlockSpec((k, tn), lambda i, j: (0, j)), pl.BlockSpec((k, LANES), lambda i, j: (0, 0))
    in_specs = [pl.BlockSpec((tm, k), lambda i, j: (i, 0)), w_spec]
    args = [x, w]
    blocks = _nbytes((tm, k), x.dtype) + _nbytes((k, tn), w.dtype) + _nbytes((tm, tn), out_dtype)
    if residual is not None:
        in_specs.append(tile)
        args.append(residual)
        blocks += _nbytes((tm, tn), residual.dtype)
    if row_ss is not None:
        in_specs.append(row_block)
        args.append(row_ss)
    if side_w is not None:
        in_specs.append(side_spec)
        args.append(side_w)
        blocks += _nbytes((k, LANES), side_w.dtype)
    gate_scale = None
    if gates is not None:
        gate_w, gate_b, gate_scale = gates
        ng = silu_from // tn
        gw = gate_w.shape[1] // ng
        assert side_w is not None and gate_w.shape[1] % ng == 0 and gw % LANES == 0
        gate_blk = lambda i, j: (0, jnp.minimum(j, ng - 1))
        in_specs += [pl.BlockSpec((LANES, gw), gate_blk), pl.BlockSpec((1, gw), gate_blk)]
        args += [gate_w, gate_b.reshape(1, -1)]
        blocks += _nbytes((LANES, gw), BF16) + 2 * _nbytes((tm, gw), BF16)
    out_specs = [tile]
    out_shape = [jax.ShapeDtypeStruct((m, n), out_dtype)]
    if stats:
        out_specs += [tile, row_block]
        out_shape += [jax.ShapeDtypeStruct((m, n), BF16), jax.ShapeDtypeStruct((m, LANES), F32)]
        blocks += _nbytes((tm, tn), BF16)
    if side_w is not None:
        out_specs.append(row_block)
        out_shape.append(jax.ShapeDtypeStruct((m, LANES), F32))
    if gates is not None:
        out_specs += [pl.BlockSpec((tm, gw), lambda i, j: (i, jnp.minimum(j, ng - 1)))] * 2
        out_shape += [jax.ShapeDtypeStruct((m, gate_w.shape[1]), BF16)] * 2
    blocks += _add_casts(casts, grid, in_specs, args, out_specs, out_shape)
    return pl.pallas_call(
        functools.partial(_mm_kernel, relu2=relu2, residual=residual is not None, scaled=row_ss is not None,
                          stats=stats, side=side_w is not None, gate_scale=gate_scale,
                          cast_gains=tuple(job.gain is not None for job in casts), inv_d=1.0 / k,
                          w_dims=(((1,), (1 if w_transposed else 0,)), ((), ())), silu_from=silu_from),
        grid=grid,
        in_specs=in_specs,
        out_specs=out_specs,
        out_shape=out_shape,
        compiler_params=pltpu.CompilerParams(
            dimension_semantics=("arbitrary", "arbitrary"),
            vmem_limit_bytes=_vmem_limit(2 * blocks + _nbytes((tm, tn), F32))),
        name=name,
    )(*args)


def _mm_ktiled_kernel(x_ref, w_ref, res_ref, o_ref, *stat_refs, nk):
    j, kk = pl.program_id(1), pl.program_id(2)
    col_tiles = [slice(c0, c0 + MXU_COLS) for c0 in range(0, o_ref.shape[1], MXU_COLS)]

    def step(first, final):
        ss_part = None
        for cols in col_tiles:
            base = res_ref[:, cols] if first else o_ref[:, cols]
            out = base + jnp.dot(x_ref[...], w_ref[:, cols], preferred_element_type=F32)
            o_ref[:, cols] = out
            if final and stat_refs:
                stat_refs[0][:, cols] = out.astype(stat_refs[0].dtype)
                part = _row_sumsq(out)
                ss_part = part if ss_part is None else ss_part + part
        if final and stat_refs:
            _accumulate_row_stats(stat_refs[1], ss_part, j == 0)

    if nk == 1:
        step(True, True)
        return
    pl.when(kk == 0)(functools.partial(step, True, False))
    if stat_refs:
        if nk > 2:
            pl.when((kk > 0) & (kk < nk - 1))(functools.partial(step, False, False))
        pl.when(kk == nk - 1)(functools.partial(step, False, True))
    else:
        pl.when(kk > 0)(functools.partial(step, False, False))


def matmul_ktiled(x, w, residual, *, stats=False, tm=1024, tn=1024, tk=4096, name="matmul_ktiled"):
    m, k = x.shape
    n = w.shape[1]
    assert m % tm == 0 and n % tn == 0 and k % tk == 0
    tile = pl.BlockSpec((tm, tn), lambda i, j, kk: (i, j))
    out_specs = [tile]
    out_shape = [jax.ShapeDtypeStruct((m, n), F32)]
    blocks = (_nbytes((tm, tk), x.dtype) + _nbytes((tk, tn), w.dtype) + 2 * _nbytes((tm, tn), F32))
    if stats:
        out_specs += [tile, pl.BlockSpec((tm, LANES), lambda i, j, kk: (i, 0))]
        out_shape += [jax.ShapeDtypeStruct((m, n), BF16), jax.ShapeDtypeStruct((m, LANES), F32)]
        blocks += _nbytes((tm, tn), BF16)
    return pl.pallas_call(
        functools.partial(_mm_ktiled_kernel, nk=k // tk),
        grid=(m // tm, n // tn, k // tk),
        in_specs=[
            pl.BlockSpec((tm, tk), lambda i, j, kk: (i, kk)),
            pl.BlockSpec((tk, tn), lambda i, j, kk: (kk, j)),
            tile,
        ],
        out_specs=out_specs,
        out_shape=out_shape,
        compiler_params=pltpu.CompilerParams(
            dimension_semantics=("arbitrary", "arbitrary", "arbitrary"),
            vmem_limit_bytes=_vmem_limit(2 * blocks + _nbytes((tm, tn), F32))),
        name=name,
    )(x, w, residual)


def _pool_kernel(*refs, tm, seq, halo, cast_gains):
    it = iter(refs)
    x_ref, xp_ref, xn_ref, gmix_ref, gmlp_ref, w_ref, scale_ref = (next(it) for _ in range(7))
    cast_in = _take_cast_refs(it, cast_gains)
    h_ref, hn_ref = next(it), next(it)
    cast_out = [next(it) for _ in cast_gains]
    buf_ref, lvl_ref = next(it), next(it)

    i = pl.program_id(1)
    nblk = pl.num_programs(1)
    gmix = gmix_ref[...]
    x = x_ref[0]
    d_model = x.shape[-1]
    group = d_model // len(POOL_WINDOWS)

    ext = tm + 2 * halo
    buf_ref[halo:halo + tm, :] = _rmsnorm_rows(x, gmix)
    buf_ref[0:halo, :] = jnp.where(i > 0, _rmsnorm_rows(xp_ref[0], gmix), 0.0)
    buf_ref[halo + tm:ext, :] = jnp.where(i < nblk - 1, _rmsnorm_rows(xn_ref[0], gmix), 0.0)
    buf_ref[ext:ext + halo, :] = jnp.zeros((halo, d_model), F32)
    lvl_ref[:, ext:ext + halo, :] = jnp.zeros((2, halo, group), F32)

    pos = i * tm + lax.broadcasted_iota(jnp.int32, (tm, 1), 0)
    for gi, win in enumerate(POOL_WINDOWS):
        cols = slice(gi * group, (gi + 1) * group)
        half = win // 2
        load = lambda lo, n: buf_ref[lo:lo + n, cols]
        span, slot = 1, 0
        while span < half:
            lvl_ref[slot, 0:ext, :] = load(0, ext) + load(span, ext)
            load = functools.partial(lambda s, lo, n: lvl_ref[s, lo:lo + n, :], slot)
            span, slot = 2 * span, 1 - slot
        acc = load(halo - half, tm) + load(halo, tm)
        count = (jnp.minimum(pos + half, seq) - jnp.maximum(pos - half, 0)).astype(F32)
        diff = acc * (1.0 / count) - buf_ref[halo:halo + tm, cols]
        y = jnp.dot(diff.astype(BF16), w_ref[gi], preferred_element_type=F32)
        h_ref[0, :, cols] = x[:, cols] + y * scale_ref[:, cols]

    hn_ref[0] = _rmsnorm_rows(h_ref[0], gmlp_ref[...]).astype(hn_ref.dtype)
    _run_casts(cast_in, cast_out)


def pool_layer(x, g_mix, g_mlp, pool_w, pool_scale, casts=(), tm=256):
    b, s, d = x.shape
    halo = SUBLANES
    assert max(POOL_WINDOWS) // 2 <= halo and tm % halo == 0 and s % tm == 0
    ng, gd, _ = pool_w.shape
    hb = tm // halo
    grid = (b, s // tm)
    row_block = pl.BlockSpec((1, tm, d), lambda bi, i: (bi, i, 0))
    vec = pl.BlockSpec((1, d), lambda bi, i: (0, 0))
    in_specs = [
        row_block,
        pl.BlockSpec((1, halo, d), lambda bi, i: (bi, jnp.maximum(i * hb - 1, 0), 0)),
        pl.BlockSpec((1, halo, d), lambda bi, i: (bi, jnp.minimum((i + 1) * hb, s // halo - 1), 0)),
        vec,
        vec,
        pl.BlockSpec((ng, gd, gd), lambda bi, i: (0, 0, 0), pipeline_mode=pl.Buffered(1)),
        vec,
    ]
    args = [x, x, x, g_mix.reshape(1, d), g_mlp.reshape(1, d), pool_w.astype(BF16), pool_scale.reshape(1, d)]
    out_specs = [row_block, row_block]
    out_shape = [jax.ShapeDtypeStruct((b, s, d), F32), jax.ShapeDtypeStruct((b, s, d), BF16)]
    blocks = (2 * _nbytes((tm, d), F32) + _nbytes((tm, d), BF16)) * 2 + _nbytes(pool_w.shape, BF16)
    blocks += 2 * _add_casts(casts, grid, in_specs, args, out_specs, out_shape)
    scratch = [pltpu.VMEM((tm + 3 * halo, d), F32), pltpu.VMEM((2, tm + 3 * halo, gd), F32)]
    scratch_bytes = _nbytes((tm + 3 * halo, d + 2 * gd), F32)
    return pl.pallas_call(
        functools.partial(_pool_kernel, tm=tm, seq=s, halo=halo,
                          cast_gains=tuple(job.gain is not None for job in casts)),
        grid=grid,
        in_specs=in_specs,
        out_specs=out_specs,
        out_shape=out_shape,
        scratch_shapes=scratch,
        compiler_params=pltpu.CompilerParams(
            dimension_semantics=("arbitrary", "arbitrary"),
            vmem_limit_bytes=_vmem_limit(blocks + scratch_bytes)),
        name="pool_layer",
    )(*args)


def _gla_block_stages(q_ref, k_ref, v_ref, ghi_ref, glo_ref, state_ref, finish, *, head, row0, rows, dk, dv,
                       reverse):
    c = CHUNK
    ns = rows // c
    order = tuple(reversed(range(ns))) if reverse else tuple(range(ns))
    kcols = slice(head * dk, (head + 1) * dk)
    vcols = slice(head * dv, (head + 1) * dv)

    def load(ref, cols):
        return jnp.concatenate([ref[0, row0 + ci * c:row0 + (ci + 1) * c, cols] for ci in order], axis=0)

    q = load(q_ref, kcols) * (dk ** -0.5)
    k = load(k_ref, kcols)

    row = lax.broadcasted_iota(jnp.int32, (rows, rows), 0)
    col = lax.broadcasted_iota(jnp.int32, (rows, rows), 1)
    shift = c.bit_length() - 1
    row_chunk = lax.shift_right_logical(row, shift)
    col_chunk = lax.shift_right_logical(col, shift)
    causal = (col >= row) if reverse else (col <= row)
    same_chunk = (row_chunk == col_chunk) & causal
    tri = jnp.where(same_chunk, 1.0, 0.0).astype(BF16)
    b = (jnp.dot(tri, load(ghi_ref, kcols), preferred_element_type=F32)
         + jnp.dot(tri, load(glo_ref, kcols), preferred_element_type=F32))

    last_i, mid_i = (0, c - c // 2) if reverse else (c - 1, c // 2 - 1)
    b_last = [b[p * c + last_i:p * c + last_i + 1, :] for p in range(ns)]
    b_mid = [b[p * c + mid_i:p * c + mid_i + 1, :] for p in range(ns)]
    a = [jnp.exp2(x) for x in b_last]
    yield False

    def prod(factors):
        out = None
        for f in factors:
            out = f if out is None else out * f
        return out

    q_intra, k_intra, q_inter, k_state, q_prev, k_next = [], [], [], [], [], []
    q_skip = {d: [] for d in range(2, ns)}
    for p in range(ns):
        rs = slice(p * c, (p + 1) * c)
        bp, qp, kp = b[rs], q[rs], k[rs]
        qn = qp * jnp.exp2(bp)
        q_intra.append((qp * jnp.exp2(bp - b_mid[p])).astype(BF16))
        q_inter.append(qn.astype(BF16))
        before = prod(a[:p])
        q_prev.append((qn if before is None else qn * before).astype(BF16))
        for d in range(2, p + 1):
            q_skip[d].append((qn * prod(a[p - d + 1:p])).astype(BF16))
        yield False
        ks = kp * jnp.exp2(b_last[p] - bp)
        k_intra.append((kp * jnp.exp2(b_mid[p] - bp)).astype(BF16))
        k_state.append(ks.astype(BF16))
        after = prod(a[p + 1:])
        k_next.append((ks if after is None else ks * after).astype(BF16))
        yield False
    v = load(v_ref, vcols).astype(BF16)
    yield True

    nt = (((1,), (1,)), ((), ()))
    s_intra = lax.dot_general(jnp.concatenate(q_intra, axis=0), jnp.concatenate(k_intra, axis=0), nt,
                              preferred_element_type=F32)
    yield None
    cross_lhs = q_inter + [x for d in range(2, ns) for x in q_skip[d]]
    s_cross = lax.dot_general(jnp.concatenate(cross_lhs, axis=0), jnp.concatenate(k_state, axis=0), nt,
                              preferred_element_type=F32)
    yield None

    skip_base = {}
    base = ns * c
    for d in range(2, ns):
        skip_base[d] = base
        base += (ns - d) * c
    score_rows = []
    row_in = lax.broadcasted_iota(jnp.int32, (c, rows), 0)
    col_in = lax.broadcasted_iota(jnp.int32, (c, rows), 1)
    col_chunk_in = lax.shift_right_logical(col_in, shift)
    for p in range(ns):
        rs = slice(p * c, (p + 1) * c)
        col_local = col_in - p * c
        causal_in = (col_local >= row_in) if reverse else (col_local <= row_in)
        sc = jnp.where((col_chunk_in == p) & causal_in, s_intra[rs], 0.0)
        if p >= 1:
            sc = jnp.where(col_chunk_in == p - 1, s_cross[rs], sc)
        for d in range(2, p + 1):
            off = skip_base[d] + (p - d) * c
            sc = jnp.where(col_chunk_in == p - d, s_cross[off:off + c], sc)
        score_rows.append(sc.astype(BF16))
    scores = jnp.concatenate(score_rows, axis=0)
    o = jnp.dot(scores, v, preferred_element_type=F32)
    yield None

    state = state_ref[head]
    o = o + jnp.dot(jnp.concatenate(q_prev, axis=0), state.astype(BF16), preferred_element_type=F32)
    yield None

    upd = lax.dot_general(jnp.concatenate(k_next, axis=0), v, (((0,), (0,)), ((), ())),
                          preferred_element_type=F32)
    decay_t = jnp.transpose(jnp.broadcast_to(prod(a), (LANES, dk)))
    state_ref[head] = _lane_tile(decay_t, dv) * state + upd
    yield None

    for p, ci in enumerate(order):
        finish(row0 + ci * c, o[p * c:(p + 1) * c])
        yield None


def _run_staggered(blocks):
    prev = None
    for gen in blocks:
        ready = False
        while not ready:
            ready = next(gen)
            if prev is not None:
                next(prev, None)
        if prev is not None:
            for _ in prev:
                pass
        prev = gen
    for _ in prev:
        pass


def _gla_kernel(*refs, nb, heads, sub, dk, dv, cast_gains):
    it = iter(refs)
    q_ref, k_ref, v_ref, ghi_ref, glo_ref, gate_ref = (next(it) for _ in range(6))
    cast_in = _take_cast_refs(it, cast_gains)
    o_ref = next(it)
    cast_out = [next(it) for _ in cast_gains]
    state_ref, acc_ref = next(it), next(it)

    s = pl.program_id(2)
    rows = q_ref.shape[1]
    stages = functools.partial(_gla_block_stages, q_ref, k_ref, v_ref, ghi_ref, glo_ref, state_ref,
                               rows=sub, dk=dk, dv=dv)
    starts = tuple(range(0, rows, sub))

    @pl.when((s == 0) | (s == nb))
    def _():
        state_ref[...] = jnp.zeros_like(state_ref)

    @pl.when(s < nb)
    def _():
        base = pl.multiple_of(s * rows, rows)

        def keep(head, row, piece):
            acc_ref[pl.ds(base + row, CHUNK), head * dv:(head + 1) * dv] = piece

        _run_casts(cast_in, cast_out)
        _run_staggered([stages(functools.partial(keep, head), head=head, row0=row0, reverse=False)
                        for head in range(heads) for row0 in starts])

    @pl.when(s >= nb)
    def _():
        base = pl.multiple_of((2 * nb - 1 - s) * rows, rows)

        def emit(head, row, piece):
            vcols = slice(head * dv, (head + 1) * dv)
            o = acc_ref[pl.ds(base + row, CHUNK), vcols] + piece
            o = o * lax.rsqrt(jnp.mean(o * o, axis=-1, keepdims=True) + EPS)
            o_ref[0, row:row + CHUNK, vcols] = (o * gate_ref[0, row:row + CHUNK, vcols]).astype(o_ref.dtype)

        _run_casts(cast_in, cast_out)
        _run_staggered([stages(functools.partial(emit, head), head=head, row0=row0, reverse=True)
                        for head in range(heads) for row0 in reversed(starts)])


def gla_core(p, g_hi, g_lo, *, heads, casts=(), bs=1024, sub=256, heads_per_step=1):
    b, s, width = p.shape
    dv_total = width // 3
    dk_total = dv_total // 2
    dk, dv = dk_total // heads, dv_total // heads
    hp = heads_per_step
    bs = min(bs, s)
    assert heads % hp == 0 and s % bs == 0 and bs % sub == 0 and sub % CHUNK == 0
    gk, gv = hp * dk, hp * dv
    nb = s // bs
    groups = heads // hp
    kq = dk_total // gk
    kv = (2 * dk_total) // gv
    kg = (2 * dk_total + dv_total) // gv
    grid = (b, groups, 2 * nb)

    def blk(i):
        return jnp.where(i < nb, i, 2 * nb - 1 - i)

    def out_blk(i):
        return jnp.where(i < nb, nb - 1, 2 * nb - 1 - i)

    decay_spec = pl.BlockSpec((1, bs, gk), lambda bi, h, i: (bi, blk(i), (i // nb) * groups + h))
    in_specs = [
        pl.BlockSpec((1, bs, gk), lambda bi, h, i: (bi, blk(i), h)),
        pl.BlockSpec((1, bs, gk), lambda bi, h, i: (bi, blk(i), kq + h)),
        pl.BlockSpec((1, bs, gv), lambda bi, h, i: (bi, blk(i), kv + h)),
        decay_spec,
        decay_spec,
        pl.BlockSpec((1, bs, gv), lambda bi, h, i: (bi, out_blk(i), kg + h)),
    ]
    args = [p, p, p, g_hi, g_lo, p]
    out_specs = [pl.BlockSpec((1, bs, gv), lambda bi, h, i: (bi, out_blk(i), h))]
    out_shape = [jax.ShapeDtypeStruct((b, s, dv_total), BF16)]
    blocks = (2 * _nbytes((bs, gk), F32) + 2 * _nbytes((bs, gv), F32) + 2 * _nbytes((bs, gk), BF16)
              + _nbytes((bs, gv), BF16))
    blocks += _add_casts(casts, grid, in_specs, args, out_specs, out_shape)
    scratch_bytes = _nbytes((hp, dk, dv), F32) + _nbytes((s, gv), F32)
    return pl.pallas_call(
        functools.partial(_gla_kernel, nb=nb, heads=hp, sub=sub, dk=dk, dv=dv,
                          cast_gains=tuple(job.gain is not None for job in casts)),
        grid=grid,
        in_specs=in_specs,
        out_specs=out_specs,
        out_shape=out_shape,
        scratch_shapes=[pltpu.VMEM((hp, dk, dv), F32), pltpu.VMEM((s, gv), F32)],
        compiler_params=pltpu.CompilerParams(
            dimension_semantics=("arbitrary", "arbitrary", "arbitrary"),
            vmem_limit_bytes=_vmem_limit(2 * blocks + scratch_bytes)),
        name="gla_scan",
    )(*args)


def gla_layer(h, hb, h_ss, w_in_t, w_out_f32, layer, w_up_f, b_up_f, w_up_b, b_up_b, g_norm, next_cast):
    b, s, d = h.shape
    t = b * s
    main = w_in_t.shape[0] - 2 * GATE_RANK
    h2 = h.reshape(t, d)
    w_r_t = jnp.pad(w_in_t[main:], ((0, LANES - 2 * GATE_RANK), (0, 0)))
    gate_w = jnp.concatenate([jnp.pad(w_up_f, ((0, LANES - GATE_RANK), (0, 0))),
                              jnp.pad(w_up_b, ((GATE_RANK, LANES - 2 * GATE_RANK), (0, 0)))], axis=1).astype(BF16)
    gate_b = jnp.concatenate([b_up_f, b_up_b])
    p, _, g_hi, g_lo, w_out = matmul(
        hb, w_in_t, F32, n=main, w_transposed=True, silu_from=main - w_out_f32.shape[1], row_ss=h_ss, side_w=w_r_t,
        gates=(gate_w, gate_b, 1.0 / GATE_TAU),
        casts=(CastJob(w_out_f32, layer, gain=jnp.tile(g_norm, GLA_HEADS)),), name="gla_in_proj")
    dk2 = gate_w.shape[1]
    og, next_bf16 = gla_core(p.reshape(b, s, main), g_hi.reshape(b, s, dk2), g_lo.reshape(b, s, dk2),
                             heads=GLA_HEADS, casts=(next_cast,))
    out, out_b, out_ss = matmul(og.reshape(t, -1), w_out, F32, residual=h2, stats=True, name="gla_out_proj")
    return out, out_b, out_ss, next_bf16


def kernel(x, norm_mix, norm_mlp, norm_final, pool_w, pool_scale, gla_w_in, gla_w_up_f, gla_b_up_f, gla_w_up_b,
           gla_b_up_b, gla_g_norm, gla_w_out, mlp_w_in, mlp_w_out):
    b, s, d = x.shape
    t = b * s
    w_in_t_f32 = jnp.swapaxes(gla_w_in, 1, 2)
    h, hn, w1 = pool_layer(x, norm_mix[0], norm_mlp[0], pool_w[0], pool_scale[0], casts=(CastJob(mlp_w_in, 0),))
    a, w2, w_in = matmul(hn.reshape(t, d), w1, BF16, relu2=True,
                         casts=(CastJob(mlp_w_out, 0), CastJob(w_in_t_f32, 0, gain=norm_mix[1], gain_axis=1)),
                         name="mlp_up_proj")
    h, hb, h_ss = matmul_ktiled(a, w2, h.reshape(t, d), stats=True, name="mlp_down_proj")

    h, hb, h_ss, w1 = gla_layer(h.reshape(b, s, d), hb, h_ss, w_in, gla_w_out, 0, gla_w_up_f[0], gla_b_up_f[0],
                                gla_w_up_b[0], gla_b_up_b[0], gla_g_norm[0],
                                CastJob(mlp_w_in, 1, gain=norm_mlp[1]))

    a, w2 = matmul(hb, w1, BF16, relu2=True, row_ss=h_ss, casts=(CastJob(mlp_w_out, 1),), name="mlp_up_proj")
    (h,) = matmul_ktiled(a, w2, h, name="mlp_down_proj")
    return rmsnorm(h, norm_final, F32).reshape(b, s, d)
```

```python
import functools
import math
from typing import NamedTuple, Optional

import jax
import jax.numpy as jnp
from jax import lax
from jax.experimental import pallas as pl
from jax.experimental.pallas import tpu as pltpu

EPS = 1e-6
POOL_WINDOWS = (2, 4, 8, 16)
GLA_HEADS = 4
GATE_RANK = 16
GATE_TAU = 16.0
CHUNK = 64

V7X_VMEM_BYTES = 64 * 1024 * 1024
VMEM_UNSCOPED_BYTES = 4 * 1024 * 1024
LANES = 128
SUBLANES = 8
MXU_COLS = 256

F32 = jnp.float32
BF16 = jnp.bfloat16


def _vmem_limit(block_bytes):
    return int(min(V7X_VMEM_BYTES - VMEM_UNSCOPED_BYTES, 2 * block_bytes))


def _nbytes(shape, dtype):
    n = 1
    for s in shape:
        n *= s
    return n * jnp.dtype(dtype).itemsize


def _lane_tile(x, n):
    return jnp.concatenate([x] * (n // LANES), axis=1)


def _row_sumsq(x):
    return jnp.broadcast_to(jnp.sum(x * x, axis=-1, keepdims=True), (x.shape[0], LANES))


def _rmsnorm_rows(x, g):
    y = x * lax.rsqrt(jnp.mean(x * x, axis=-1, keepdims=True) + EPS)
    return y * g


class CastJob(NamedTuple):
    src: jax.Array
    layer: int
    gain: Optional[jax.Array] = None
    gain_axis: int = 0


class _CastPlan(NamedTuple):
    in_specs: list
    args: list
    out_spec: pl.BlockSpec
    out_shape: jax.ShapeDtypeStruct
    nbytes: int


def _plan_cast(job, grid):
    _, r, c = job.src.shape
    steps = 1
    for g in grid:
        steps *= g
    bf16_rows = 2 * SUBLANES
    nslabs = next(n for n in range(min(steps, r // bf16_rows), 0, -1) if r % (n * bf16_rows) == 0)
    slab = r // nslabs

    def slab_of(*idx):
        step = 0
        for g, i in zip(grid, idx):
            step = step * g + i
        return step * nslabs // steps

    layer = job.layer
    in_specs = [pl.BlockSpec((None, slab, c), lambda *idx: (layer, slab_of(*idx), 0))]
    args = [job.src]
    if job.gain is not None and job.gain_axis == 0:
        in_specs.append(pl.BlockSpec((slab, 1), lambda *idx: (slab_of(*idx), 0)))
        args.append(job.gain.reshape(r, 1))
    elif job.gain is not None:
        in_specs.append(pl.BlockSpec((1, c), lambda *idx: (0, 0)))
        args.append(job.gain.reshape(1, c))
    return _CastPlan(in_specs, args, pl.BlockSpec((slab, c), lambda *idx: (slab_of(*idx), 0)),
                     jax.ShapeDtypeStruct((r, c), BF16), _nbytes((slab, c), F32) + _nbytes((slab, c), BF16))


def _take_cast_refs(it, gains):
    return [(next(it), next(it) if g else None) for g in gains]


def _run_casts(cast_in, cast_out, piece=0, pieces=1):
    for (src, gain), dst in zip(cast_in, cast_out):
        width = dst.shape[1] // pieces
        assert width % LANES == 0
        cols = slice(piece * width, (piece + 1) * width)
        val = src[:, cols]
        if gain is not None:
            val = val * (gain[:, cols] if gain.shape[1] == dst.shape[1] else gain[...])
        dst[:, cols] = val.astype(dst.dtype)


def _add_casts(casts, grid, in_specs, args, out_specs, out_shape):
    nbytes = 0
    for job in casts:
        plan = _plan_cast(job, grid)
        in_specs.extend(plan.in_specs)
        args.extend(plan.args)
        out_specs.append(plan.out_spec)
        out_shape.append(plan.out_shape)
        nbytes += plan.nbytes
    return nbytes


def _rmsnorm_kernel(x_ref, g_ref, o_ref):
    o_ref[...] = _rmsnorm_rows(x_ref[...], g_ref[...]).astype(o_ref.dtype)


def rmsnorm(x, g, out_dtype, tm=512):
    m, d = x.shape
    blocks = 2 * (_nbytes((tm, d), x.dtype) + _nbytes((tm, d), out_dtype))
    return pl.pallas_call(
        _rmsnorm_kernel,
        grid=(m // tm,),
        in_specs=[pl.BlockSpec((tm, d), lambda i: (i, 0)), pl.BlockSpec((1, d), lambda i: (0, 0))],
        out_specs=pl.BlockSpec((tm, d), lambda i: (i, 0)),
        out_shape=jax.ShapeDtypeStruct((m, d), out_dtype),
        compiler_params=pltpu.CompilerParams(
            dimension_semantics=("arbitrary",), vmem_limit_bytes=_vmem_limit(blocks)),
        name="rmsnorm",
    )(x, g.reshape(1, d))


def _accumulate_row_stats(ss_ref, part, first):
    @pl.when(first)
    def _():
        ss_ref[...] = part

    @pl.when(jnp.logical_not(first))
    def _():
        ss_ref[...] += part


def _mm_kernel(*refs, relu2, residual, scaled, stats, side, gate_scale, copy_bf16, cast_gains, inv_d, w_dims,
               silu_from):
    gates = gate_scale is not None
    it = iter(refs)
    x_ref, w_ref = next(it), next(it)
    res_ref = next(it) if residual else None
    ss_in_ref = next(it) if scaled else None
    side_w_ref = next(it) if side else None
    gate_w_ref, gate_b_ref = (next(it), next(it)) if gates else (None, None)
    cast_in = _take_cast_refs(it, cast_gains)
    o_ref = next(it)
    ob_ref, ss_out_ref = (next(it), next(it)) if stats else (None, None)
    side_o_ref = next(it) if side else None
    ghi_ref, glo_ref = (next(it), next(it)) if gates else (None, None)
    copy_ref = next(it) if copy_bf16 else None
    cast_out = [next(it) for _ in cast_gains]

    j = pl.program_id(1)
    w_transposed = w_dims[0][1][0] == 1
    row_scale = lax.rsqrt(ss_in_ref[...] * inv_d + EPS) if scaled else None
    width = MXU_COLS
    tiles = o_ref.shape[1] // width

    if side:
        @pl.when(j == 0)
        def _():
            extra = lax.dot_general(x_ref[...], side_w_ref[...], w_dims, preferred_element_type=F32)
            side_o_ref[...] = extra * row_scale if scaled else extra

    def gate_block(piece, pieces):
        n_rows = side_o_ref.shape[0] // pieces
        rows = slice(piece * n_rows, (piece + 1) * n_rows)
        z = jnp.dot(side_o_ref[rows, :].astype(BF16), gate_w_ref[...], preferred_element_type=F32) + gate_b_ref[...]
        z2 = z * math.log2(math.e)
        g = (jnp.minimum(z2, 0.0) - jnp.log2(1.0 + jnp.exp2(-jnp.abs(z2)))) * gate_scale
        hi = g.astype(BF16)
        ghi_ref[rows, :] = hi
        glo_ref[rows, :] = (g - hi.astype(F32)).astype(BF16)

    def body(silu):
        ss_part = None
        for tile in range(tiles):
            cols = slice(tile * width, (tile + 1) * width)
            w_tile = w_ref[cols, :] if w_transposed else w_ref[:, cols]
            acc = lax.dot_general(x_ref[...], w_tile, w_dims, preferred_element_type=F32)
            if scaled:
                acc = acc * _lane_tile(row_scale, width)
            if relu2:
                acc = jnp.maximum(acc, 0.0)
                acc = acc * acc
            if silu:
                acc = jax.nn.silu(acc)
            if residual:
                acc = res_ref[:, cols] + acc
            o_ref[:, cols] = acc.astype(o_ref.dtype)
            if copy_bf16 and not silu:
                copy_ref[:, cols] = acc.astype(copy_ref.dtype)
            if stats:
                ob_ref[:, cols] = acc.astype(ob_ref.dtype)
                part = _row_sumsq(acc)
                ss_part = part if ss_part is None else ss_part + part
            _run_casts(cast_in, cast_out, tile, tiles)
            if gates and not silu:
                gate_block(tile, tiles)
        if stats:
            _accumulate_row_stats(ss_out_ref, ss_part, j == 0)

    if silu_from is None:
        body(False)
    else:
        in_silu = j * o_ref.shape[1] >= silu_from
        pl.when(in_silu)(functools.partial(body, True))
        pl.when(jnp.logical_not(in_silu))(functools.partial(body, False))


def matmul(x, w, out_dtype, *, n=None, w_transposed=False, relu2=False, silu_from=None, residual=None, row_ss=None,
           stats=False, side_w=None, gates=None, copy_bf16=None, casts=(), tm=1024, tn=1024, name="matmul"):
    m, k = x.shape
    n = w.shape[0 if w_transposed else 1] if n is None else n
    tn = min(tn, n)
    assert m % tm == 0 and n % tn == 0 and (silu_from is None or silu_from % tn == 0)
    grid = (m // tm, n // tn)
    row_block = pl.BlockSpec((tm, LANES), lambda i, j: (i, 0))
    tile = pl.BlockSpec((tm, tn), lambda i, j: (i, j))
    if w_transposed:
        w_spec, side_spec = pl.BlockSpec((tn, k), lambda i, j: (j, 0)), pl.BlockSpec((LANES, k), lambda i, j: (0, 0))
    else:
        w_spec, side_spec = pl.BlockSpec((k, tn), lambda i, j: (0, j)), pl.BlockSpec((k, LANES), lambda i, j: (0, 0))
    in_specs = [pl.BlockSpec((tm, k), lambda i, j: (i, 0)), w_spec]
    args = [x, w]
    blocks = _nbytes((tm, k), x.dtype) + _nbytes((k, tn), w.dtype) + _nbytes((tm, tn), out_dtype)
    if residual is not None:
        in_specs.append(tile)
        args.append(residual)
        blocks += _nbytes((tm, tn), residual.dtype)
    if row_ss is not None:
        in_specs.append(row_block)
        args.append(row_ss)
    if side_w is not None:
        in_specs.append(side_spec)
        args.append(side_w)
        blocks += _nbytes((k, LANES), side_w.dtype)
    gate_scale = None
    if gates is not None:
        gate_w, gate_b, gate_scale = gates
        ng = silu_from // tn
        gw = gate_w.shape[1] // ng
        assert side_w is not None and gate_w.shape[1] % ng == 0 and gw % LANES == 0
        gate_blk = lambda i, j: (0, jnp.minimum(j, ng - 1))
        in_specs += [pl.BlockSpec((LANES, gw), gate_blk), pl.BlockSpec((1, gw), gate_blk)]
        args += [gate_w, gate_b.reshape(1, -1)]
        blocks += _nbytes((LANES, gw), BF16) + 2 * _nbytes((tm, gw), BF16)
    out_specs = [tile]
    out_shape = [jax.ShapeDtypeStruct((m, n), out_dtype)]
    if stats:
        out_specs += [tile, row_block]
        out_shape += [jax.ShapeDtypeStruct((m, n), BF16), jax.ShapeDtypeStruct((m, LANES), F32)]
        blocks += _nbytes((tm, tn), BF16)
    if side_w is not None:
        out_specs.append(row_block)
        out_shape.append(jax.ShapeDtypeStruct((m, LANES), F32))
    if gates is not None:
        out_specs += [pl.BlockSpec((tm, gw), lambda i, j: (i, jnp.minimum(j, ng - 1)))] * 2
        out_shape += [jax.ShapeDtypeStruct((m, gate_w.shape[1]), BF16)] * 2
    if copy_bf16 is not None:
        first, width = copy_bf16
        assert first % tn == 0 and width % tn == 0 and first + width == silu_from
        lo, hi = first // tn, (first + width) // tn - 1
        out_specs.append(pl.BlockSpec((tm, tn), lambda i, j: (i, jnp.clip(j, lo, hi) - lo)))
        out_shape.append(jax.ShapeDtypeStruct((m, width), BF16))
        blocks += _nbytes((tm, tn), BF16)
    blocks += _add_casts(casts, grid, in_specs, args, out_specs, out_shape)
    return pl.pallas_call(
        functools.partial(_mm_kernel, relu2=relu2, residual=residual is not None, scaled=row_ss is not None,
                          stats=stats, side=side_w is not None, gate_scale=gate_scale,
                          copy_bf16=copy_bf16 is not None,
                          cast_gains=tuple(job.gain is not None for job in casts), inv_d=1.0 / k,
                          w_dims=(((1,), (1 if w_transposed else 0,)), ((), ())), silu_from=silu_from),
        grid=grid,
        in_specs=in_specs,
        out_specs=out_specs,
        out_shape=out_shape,
        compiler_params=pltpu.CompilerParams(
            dimension_semantics=("arbitrary", "arbitrary"),
            vmem_limit_bytes=_vmem_limit(2 * blocks + _nbytes((tm, tn), F32))),
        name=name,
    )(*args)


def _mm_ktiled_kernel(x_ref, w_ref, res_ref, o_ref, *stat_refs, nk):
    j, kk = pl.program_id(1), pl.program_id(2)
    col_tiles = [slice(c0, c0 + MXU_COLS) for c0 in range(0, o_ref.shape[1], MXU_COLS)]

    def step(first, final):
        ss_part = None
        for cols in col_tiles:
            base = res_ref[:, cols] if first else o_ref[:, cols]
            out = base + jnp.dot(x_ref[...], w_ref[:, cols], preferred_element_type=F32)
            o_ref[:, cols] = out
            if final and stat_refs:
                stat_refs[0][:, cols] = out.astype(stat_refs[0].dtype)
                part = _row_sumsq(out)
                ss_part = part if ss_part is None else ss_part + part
        if final and stat_refs:
            _accumulate_row_stats(stat_refs[1], ss_part, j == 0)

    if nk == 1:
        step(True, True)
        return
    pl.when(kk == 0)(functools.partial(step, True, False))
    if stat_refs:
        if nk > 2:
            pl.when((kk > 0) & (kk < nk - 1))(functools.partial(step, False, False))
        pl.when(kk == nk - 1)(functools.partial(step, False, True))
    else:
        pl.when(kk > 0)(functools.partial(step, False, False))


def matmul_ktiled(x, w, residual, *, stats=False, tm=1024, tn=1024, tk=4096, name="matmul_ktiled"):
    m, k = x.shape
    n = w.shape[1]
    assert m % tm == 0 and n % tn == 0 and k % tk == 0
    tile = pl.BlockSpec((tm, tn), lambda i, j, kk: (i, j))
    out_specs = [tile]
    out_shape = [jax.ShapeDtypeStruct((m, n), F32)]
    blocks = (_nbytes((tm, tk), x.dtype) + _nbytes((tk, tn), w.dtype) + 2 * _nbytes((tm, tn), F32))
    if stats:
        out_specs += [tile, pl.BlockSpec((tm, LANES), lambda i, j, kk: (i, 0))]
        out_shape += [jax.ShapeDtypeStruct((m, n), BF16), jax.ShapeDtypeStruct((m, LANES), F32)]
        blocks += _nbytes((tm, tn), BF16)
    return pl.pallas_call(
        functools.partial(_mm_ktiled_kernel, nk=k // tk),
        grid=(m // tm, n // tn, k // tk),
        in_specs=[
            pl.BlockSpec((tm, tk), lambda i, j, kk: (i, kk)),
            pl.BlockSpec((tk, tn), lambda i, j, kk: (kk, j)),
            tile,
        ],
        out_specs=out_specs,
        out_shape=out_shape,
        compiler_params=pltpu.CompilerParams(
            dimension_semantics=("arbitrary", "arbitrary", "arbitrary"),
            vmem_limit_bytes=_vmem_limit(2 * blocks + _nbytes((tm, tn), F32))),
        name=name,
    )(x, w, residual)


def _pool_kernel(*refs, tm, seq, halo, cast_gains):
    it = iter(refs)
    x_ref, xp_ref, xn_ref, gmix_ref, gmlp_ref, w_ref, scale_ref = (next(it) for _ in range(7))
    cast_in = _take_cast_refs(it, cast_gains)
    h_ref, hn_ref = next(it), next(it)
    cast_out = [next(it) for _ in cast_gains]
    buf_ref, lvl_ref = next(it), next(it)

    i = pl.program_id(1)
    nblk = pl.num_programs(1)
    gmix = gmix_ref[...]
    x = x_ref[0]
    d_model = x.shape[-1]
    group = d_model // len(POOL_WINDOWS)

    ext = tm + 2 * halo
    buf_ref[halo:halo + tm, :] = _rmsnorm_rows(x, gmix)
    buf_ref[0:halo, :] = jnp.where(i > 0, _rmsnorm_rows(xp_ref[0], gmix), 0.0)
    buf_ref[halo + tm:ext, :] = jnp.where(i < nblk - 1, _rmsnorm_rows(xn_ref[0], gmix), 0.0)
    buf_ref[ext:ext + halo, :] = jnp.zeros((halo, d_model), F32)
    lvl_ref[:, ext:ext + halo, :] = jnp.zeros((2, halo, group), F32)

    pos = i * tm + lax.broadcasted_iota(jnp.int32, (tm, 1), 0)
    for gi, win in enumerate(POOL_WINDOWS):
        cols = slice(gi * group, (gi + 1) * group)
        half = win // 2
        load = lambda lo, n: buf_ref[lo:lo + n, cols]
        span, slot = 1, 0
        while span < half:
            lvl_ref[slot, 0:ext, :] = load(0, ext) + load(span, ext)
            load = functools.partial(lambda s, lo, n: lvl_ref[s, lo:lo + n, :], slot)
            span, slot = 2 * span, 1 - slot
        acc = load(halo - half, tm) + load(halo, tm)
        count = (jnp.minimum(pos + half, seq) - jnp.maximum(pos - half, 0)).astype(F32)
        diff = acc * (1.0 / count) - buf_ref[halo:halo + tm, cols]
        y = jnp.dot(diff.astype(BF16), w_ref[gi], preferred_element_type=F32)
        h_ref[0, :, cols] = x[:, cols] + y * scale_ref[:, cols]

    hn_ref[0] = _rmsnorm_rows(h_ref[0], gmlp_ref[...]).astype(hn_ref.dtype)
    _run_casts(cast_in, cast_out)


def pool_layer(x, g_mix, g_mlp, pool_w, pool_scale, casts=(), tm=256):
    b, s, d = x.shape
    halo = SUBLANES
    assert max(POOL_WINDOWS) // 2 <= halo and tm % halo == 0 and s % tm == 0
    ng, gd, _ = pool_w.shape
    hb = tm // halo
    grid = (b, s // tm)
    row_block = pl.BlockSpec((1, tm, d), lambda bi, i: (bi, i, 0))
    vec = pl.BlockSpec((1, d), lambda bi, i: (0, 0))
    in_specs = [
        row_block,
        pl.BlockSpec((1, halo, d), lambda bi, i: (bi, jnp.maximum(i * hb - 1, 0), 0)),
        pl.BlockSpec((1, halo, d), lambda bi, i: (bi, jnp.minimum((i + 1) * hb, s // halo - 1), 0)),
        vec,
        vec,
        pl.BlockSpec((ng, gd, gd), lambda bi, i: (0, 0, 0), pipeline_mode=pl.Buffered(1)),
        vec,
    ]
    args = [x, x, x, g_mix.reshape(1, d), g_mlp.reshape(1, d), pool_w.astype(BF16), pool_scale.reshape(1, d)]
    out_specs = [row_block, row_block]
    out_shape = [jax.ShapeDtypeStruct((b, s, d), F32), jax.ShapeDtypeStruct((b, s, d), BF16)]
    blocks = (2 * _nbytes((tm, d), F32) + _nbytes((tm, d), BF16)) * 2 + _nbytes(pool_w.shape, BF16)
    blocks += 2 * _add_casts(casts, grid, in_specs, args, out_specs, out_shape)
    scratch = [pltpu.VMEM((tm + 3 * halo, d), F32), pltpu.VMEM((2, tm + 3 * halo, gd), F32)]
    scratch_bytes = _nbytes((tm + 3 * halo, d + 2 * gd), F32)
    return pl.pallas_call(
        functools.partial(_pool_kernel, tm=tm, seq=s, halo=halo,
                          cast_gains=tuple(job.gain is not None for job in casts)),
        grid=grid,
        in_specs=in_specs,
        out_specs=out_specs,
        out_shape=out_shape,
        scratch_shapes=scratch,
        compiler_params=pltpu.CompilerParams(
            dimension_semantics=("arbitrary", "arbitrary"),
            vmem_limit_bytes=_vmem_limit(blocks + scratch_bytes)),
        name="pool_layer",
    )(*args)


def _gla_block_stages(q_ref, k_ref, v_ref, ghi_ref, glo_ref, state_ref, finish, *, head, row0, rows, dk, dv,
                       reverse):
    c = CHUNK
    ns = rows // c
    order = tuple(reversed(range(ns))) if reverse else tuple(range(ns))
    kcols = slice(head * dk, (head + 1) * dk)
    vcols = slice(head * dv, (head + 1) * dv)

    def load(ref, cols):
        return jnp.concatenate([ref[0, row0 + ci * c:row0 + (ci + 1) * c, cols] for ci in order], axis=0)

    q = load(q_ref, kcols) * (dk ** -0.5)
    k = load(k_ref, kcols)

    row = lax.broadcasted_iota(jnp.int32, (rows, rows), 0)
    col = lax.broadcasted_iota(jnp.int32, (rows, rows), 1)
    shift = c.bit_length() - 1
    row_chunk = lax.shift_right_logical(row, shift)
    col_chunk = lax.shift_right_logical(col, shift)
    causal = (col >= row) if reverse else (col <= row)
    same_chunk = (row_chunk == col_chunk) & causal
    tri = jnp.where(same_chunk, 1.0, 0.0).astype(BF16)
    b = (jnp.dot(tri, load(ghi_ref, kcols), preferred_element_type=F32)
         + jnp.dot(tri, load(glo_ref, kcols), preferred_element_type=F32))

    last_i, mid_i = (0, c - c // 2) if reverse else (c - 1, c // 2 - 1)
    b_last = [b[p * c + last_i:p * c + last_i + 1, :] for p in range(ns)]
    b_mid = [b[p * c + mid_i:p * c + mid_i + 1, :] for p in range(ns)]
    a = [jnp.exp2(x) for x in b_last]
    yield False

    def prod(factors):
        out = None
        for f in factors:
            out = f if out is None else out * f
        return out

    q_intra, k_intra, q_inter, k_state, q_prev, k_next = [], [], [], [], [], []
    q_skip = {d: [] for d in range(2, ns)}
    for p in range(ns):
        rs = slice(p * c, (p + 1) * c)
        bp, qp, kp = b[rs], q[rs], k[rs]
        qn = qp * jnp.exp2(bp)
        q_intra.append((qp * jnp.exp2(bp - b_mid[p])).astype(BF16))
        q_inter.append(qn.astype(BF16))
        before = prod(a[:p])
        q_prev.append((qn if before is None else qn * before).astype(BF16))
        for d in range(2, p + 1):
            q_skip[d].append((qn * prod(a[p - d + 1:p])).astype(BF16))
        yield False
        ks = kp * jnp.exp2(b_last[p] - bp)
        k_intra.append((kp * jnp.exp2(b_mid[p] - bp)).astype(BF16))
        k_state.append(ks.astype(BF16))
        after = prod(a[p + 1:])
        k_next.append((ks if after is None else ks * after).astype(BF16))
        yield False
    v = load(v_ref, vcols)
    yield True

    nt = (((1,), (1,)), ((), ()))
    s_intra = lax.dot_general(jnp.concatenate(q_intra, axis=0), jnp.concatenate(k_intra, axis=0), nt,
                              preferred_element_type=F32)
    yield None
    cross_lhs = q_inter + [x for d in range(2, ns) for x in q_skip[d]]
    s_cross = lax.dot_general(jnp.concatenate(cross_lhs, axis=0), jnp.concatenate(k_state, axis=0), nt,
                              preferred_element_type=F32)
    yield None

    skip_base = {}
    base = ns * c
    for d in range(2, ns):
        skip_base[d] = base
        base += (ns - d) * c
    score_rows = []
    row_in = lax.broadcasted_iota(jnp.int32, (c, rows), 0)
    col_in = lax.broadcasted_iota(jnp.int32, (c, rows), 1)
    col_chunk_in = lax.shift_right_logical(col_in, shift)
    for p in range(ns):
        rs = slice(p * c, (p + 1) * c)
        col_local = col_in - p * c
        causal_in = (col_local >= row_in) if reverse else (col_local <= row_in)
        sc = jnp.where((col_chunk_in == p) & causal_in, s_intra[rs], 0.0)
        if p >= 1:
            sc = jnp.where(col_chunk_in == p - 1, s_cross[rs], sc)
        for d in range(2, p + 1):
            off = skip_base[d] + (p - d) * c
            sc = jnp.where(col_chunk_in == p - d, s_cross[off:off + c], sc)
        score_rows.append(sc.astype(BF16))
    scores = jnp.concatenate(score_rows, axis=0)
    o = jnp.dot(scores, v, preferred_element_type=F32)
    yield None

    state = state_ref[head]
    o = o + jnp.dot(jnp.concatenate(q_prev, axis=0), state.astype(BF16), preferred_element_type=F32)
    yield None

    upd = lax.dot_general(jnp.concatenate(k_next, axis=0), v, (((0,), (0,)), ((), ())),
                          preferred_element_type=F32)
    decay_t = jnp.transpose(jnp.broadcast_to(prod(a), (LANES, dk)))
    state_ref[head] = _lane_tile(decay_t, dv) * state + upd
    yield None

    for p, ci in enumerate(order):
        finish(row0 + ci * c, o[p * c:(p + 1) * c])
        yield None


def _run_staggered(blocks):
    prev = None
    for gen in blocks:
        ready = False
        while not ready:
            ready = next(gen)
            if prev is not None:
                next(prev, None)
        if prev is not None:
            for _ in prev:
                pass
        prev = gen
    for _ in prev:
        pass


def _gla_kernel(*refs, nb, heads, sub, dk, dv, cast_gains):
    it = iter(refs)
    q_ref, k_ref, v_ref, ghi_ref, glo_ref, gate_ref = (next(it) for _ in range(6))
    cast_in = _take_cast_refs(it, cast_gains)
    o_ref = next(it)
    cast_out = [next(it) for _ in cast_gains]
    state_ref, acc_ref = next(it), next(it)

    s = pl.program_id(2)
    rows = q_ref.shape[1]
    stages = functools.partial(_gla_block_stages, q_ref, k_ref, v_ref, ghi_ref, glo_ref, state_ref,
                               rows=sub, dk=dk, dv=dv)
    starts = tuple(range(0, rows, sub))

    @pl.when((s == 0) | (s == nb))
    def _():
        state_ref[...] = jnp.zeros_like(state_ref)

    @pl.when(s < nb)
    def _():
        base = pl.multiple_of(s * rows, rows)

        def keep(head, row, piece):
            acc_ref[pl.ds(base + row, CHUNK), head * dv:(head + 1) * dv] = piece

        _run_casts(cast_in, cast_out)
        _run_staggered([stages(functools.partial(keep, head), head=head, row0=row0, reverse=False)
                        for head in range(heads) for row0 in starts])

    @pl.when(s >= nb)
    def _():
        base = pl.multiple_of((2 * nb - 1 - s) * rows, rows)

        def emit(head, row, piece):
            vcols = slice(head * dv, (head + 1) * dv)
            o = acc_ref[pl.ds(base + row, CHUNK), vcols] + piece
            o = o * lax.rsqrt(jnp.mean(o * o, axis=-1, keepdims=True) + EPS)
            o_ref[0, row:row + CHUNK, vcols] = (o * gate_ref[0, row:row + CHUNK, vcols]).astype(o_ref.dtype)

        _run_casts(cast_in, cast_out)
        _run_staggered([stages(functools.partial(emit, head), head=head, row0=row0, reverse=True)
                        for head in range(heads) for row0 in reversed(starts)])


def gla_core(p, v, g_hi, g_lo, *, heads, casts=(), bs=1024, sub=256, heads_per_step=1):
    b, s, width = p.shape
    dv_total = width // 3
    dk_total = dv_total // 2
    dk, dv = dk_total // heads, dv_total // heads
    hp = heads_per_step
    bs = min(bs, s)
    assert heads % hp == 0 and s % bs == 0 and bs % sub == 0 and sub % CHUNK == 0
    gk, gv = hp * dk, hp * dv
    nb = s // bs
    groups = heads // hp
    kq = dk_total // gk
    kg = (2 * dk_total + dv_total) // gv
    grid = (b, groups, 2 * nb)

    def blk(i):
        return jnp.where(i < nb, i, 2 * nb - 1 - i)

    def out_blk(i):
        return jnp.where(i < nb, nb - 1, 2 * nb - 1 - i)

    decay_spec = pl.BlockSpec((1, bs, gk), lambda bi, h, i: (bi, blk(i), (i // nb) * groups + h))
    in_specs = [
        pl.BlockSpec((1, bs, gk), lambda bi, h, i: (bi, blk(i), h)),
        pl.BlockSpec((1, bs, gk), lambda bi, h, i: (bi, blk(i), kq + h)),
        pl.BlockSpec((1, bs, gv), lambda bi, h, i: (bi, blk(i), h)),
        decay_spec,
        decay_spec,
        pl.BlockSpec((1, bs, gv), lambda bi, h, i: (bi, out_blk(i), kg + h)),
    ]
    args = [p, p, v, g_hi, g_lo, p]
    out_specs = [pl.BlockSpec((1, bs, gv), lambda bi, h, i: (bi, out_blk(i), h))]
    out_shape = [jax.ShapeDtypeStruct((b, s, dv_total), BF16)]
    blocks = (2 * _nbytes((bs, gk), F32) + _nbytes((bs, gv), F32) + 2 * _nbytes((bs, gk), BF16)
              + 2 * _nbytes((bs, gv), BF16))
    blocks += _add_casts(casts, grid, in_specs, args, out_specs, out_shape)
    scratch_bytes = _nbytes((hp, dk, dv), F32) + _nbytes((s, gv), F32)
    return pl.pallas_call(
        functools.partial(_gla_kernel, nb=nb, heads=hp, sub=sub, dk=dk, dv=dv,
                          cast_gains=tuple(job.gain is not None for job in casts)),
        grid=grid,
        in_specs=in_specs,
        out_specs=out_specs,
        out_shape=out_shape,
        scratch_shapes=[pltpu.VMEM((hp, dk, dv), F32), pltpu.VMEM((s, gv), F32)],
        compiler_params=pltpu.CompilerParams(
            dimension_semantics=("arbitrary", "arbitrary", "arbitrary"),
            vmem_limit_bytes=_vmem_limit(2 * blocks + scratch_bytes)),
        name="gla_scan",
    )(*args)


def gla_layer(h, hb, h_ss, w_in_t, w_out_f32, layer, w_up_f, b_up_f, w_up_b, b_up_b, g_norm, next_cast):
    b, s, d = h.shape
    t = b * s
    main = w_in_t.shape[0] - 2 * GATE_RANK
    h2 = h.reshape(t, d)
    w_r_t = jnp.pad(w_in_t[main:], ((0, LANES - 2 * GATE_RANK), (0, 0)))
    gate_w = jnp.concatenate([jnp.pad(w_up_f, ((0, LANES - GATE_RANK), (0, 0))),
                              jnp.pad(w_up_b, ((GATE_RANK, LANES - 2 * GATE_RANK), (0, 0)))], axis=1).astype(BF16)
    gate_b = jnp.concatenate([b_up_f, b_up_b])
    dv_total = w_out_f32.shape[1]
    dk2 = gate_w.shape[1]
    p, _, g_hi, g_lo, v, w_out, next_bf16 = matmul(
        hb, w_in_t, F32, n=main, w_transposed=True, silu_from=main - dv_total, row_ss=h_ss, side_w=w_r_t,
        gates=(gate_w, gate_b, 1.0 / GATE_TAU), copy_bf16=(dk2, dv_total),
        casts=(CastJob(w_out_f32, layer, gain=jnp.tile(g_norm, GLA_HEADS)), next_cast), name="gla_in_proj")
    og = gla_core(p.reshape(b, s, main), v.reshape(b, s, dv_total), g_hi.reshape(b, s, dk2),
                  g_lo.reshape(b, s, dk2), heads=GLA_HEADS)[0]
    out, out_b, out_ss = matmul(og.reshape(t, -1), w_out, F32, residual=h2, stats=True, name="gla_out_proj")
    return out, out_b, out_ss, next_bf16


def kernel(x, norm_mix, norm_mlp, norm_final, pool_w, pool_scale, gla_w_in, gla_w_up_f, gla_b_up_f, gla_w_up_b,
           gla_b_up_b, gla_g_norm, gla_w_out, mlp_w_in, mlp_w_out):
    b, s, d = x.shape
    t = b * s
    w_in_t_f32 = jnp.swapaxes(gla_w_in, 1, 2)
    h, hn, w1 = pool_layer(x, norm_mix[0], norm_mlp[0], pool_w[0], pool_scale[0], casts=(CastJob(mlp_w_in, 0),))
    a, w2, w_in = matmul(hn.reshape(t, d), w1, BF16, relu2=True,
                         casts=(CastJob(mlp_w_out, 0), CastJob(w_in_t_f32, 0, gain=norm_mix[1], gain_axis=1)),
                         name="mlp_up_proj")
    h, hb, h_ss = matmul_ktiled(a, w2, h.reshape(t, d), stats=True, name="mlp_down_proj")

    h, hb, h_ss, w1 = gla_layer(h.reshape(b, s, d), hb, h_ss, w_in, gla_w_out, 0, gla_w_up_f[0], gla_b_up_f[0],
                                gla_w_up_b[0], gla_b_up_b[0], gla_g_norm[0],
                                CastJob(mlp_w_in, 1, gain=norm_mlp[1]))

    a, w2 = matmul(hb, w1, BF16, relu2=True, row_ss=h_ss, casts=(CastJob(mlp_w_out, 1),), name="mlp_up_proj")
    (h,) = matmul_ktiled(a, w2, h, name="mlp_down_proj")
    return rmsnorm(h, norm_final, F32).reshape(b, s, d)
```

```python
import functools
import math
from typing import NamedTuple, Optional

import jax
import jax.numpy as jnp
from jax import lax
from jax.experimental import pallas as pl
from jax.experimental.pallas import tpu as pltpu

EPS = 1e-6
POOL_WINDOWS = (2, 4, 8, 16)
GLA_HEADS = 4
GATE_RANK = 16
GATE_TAU = 16.0
CHUNK = 64

V7X_VMEM_BYTES = 64 * 1024 * 1024
VMEM_UNSCOPED_BYTES = 4 * 1024 * 1024
LANES = 128
SUBLANES = 8
MXU_COLS = 256
RING_SLOTS = 3

F32 = jnp.float32
BF16 = jnp.bfloat16


def _vmem_limit(block_bytes):
    return int(min(V7X_VMEM_BYTES - VMEM_UNSCOPED_BYTES, 2 * block_bytes))


def _nbytes(shape, dtype):
    n = 1
    for s in shape:
        n *= s
    return n * jnp.dtype(dtype).itemsize


def _lane_tile(x, n):
    return jnp.concatenate([x] * (n // LANES), axis=1)


def _row_sumsq(x):
    return jnp.broadcast_to(jnp.sum(x * x, axis=-1, keepdims=True), (x.shape[0], LANES))


def _rmsnorm_rows(x, g):
    y = x * lax.rsqrt(jnp.mean(x * x, axis=-1, keepdims=True) + EPS)
    return y * g


class CastJob(NamedTuple):
    src: jax.Array
    layer: int
    gain: Optional[jax.Array] = None
    gain_axis: int = 0


class _CastPlan(NamedTuple):
    in_specs: list
    args: list
    out_spec: pl.BlockSpec
    out_shape: jax.ShapeDtypeStruct
    nbytes: int


def _plan_cast(job, grid):
    _, r, c = job.src.shape
    steps = 1
    for g in grid:
        steps *= g
    bf16_rows = 2 * SUBLANES
    nslabs = next(n for n in range(min(steps, r // bf16_rows), 0, -1) if r % (n * bf16_rows) == 0)
    slab = r // nslabs

    def slab_of(*idx):
        step = 0
        for g, i in zip(grid, idx):
            step = step * g + i
        return step * nslabs // steps

    layer = job.layer
    in_specs = [pl.BlockSpec((None, slab, c), lambda *idx: (layer, slab_of(*idx), 0))]
    args = [job.src]
    if job.gain is not None and job.gain_axis == 0:
        in_specs.append(pl.BlockSpec((slab, 1), lambda *idx: (slab_of(*idx), 0)))
        args.append(job.gain.reshape(r, 1))
    elif job.gain is not None:
        in_specs.append(pl.BlockSpec((1, c), lambda *idx: (0, 0)))
        args.append(job.gain.reshape(1, c))
    return _CastPlan(in_specs, args, pl.BlockSpec((slab, c), lambda *idx: (slab_of(*idx), 0)),
                     jax.ShapeDtypeStruct((r, c), BF16), _nbytes((slab, c), F32) + _nbytes((slab, c), BF16))


def _take_cast_refs(it, gains):
    return [(next(it), next(it) if g else None) for g in gains]


def _run_casts(cast_in, cast_out, piece=0, pieces=1):
    for (src, gain), dst in zip(cast_in, cast_out):
        width = dst.shape[1] // pieces
        assert width % LANES == 0
        cols = slice(piece * width, (piece + 1) * width)
        val = src[:, cols]
        if gain is not None:
            val = val * (gain[:, cols] if gain.shape[1] == dst.shape[1] else gain[...])
        dst[:, cols] = val.astype(dst.dtype)


def _add_casts(casts, grid, in_specs, args, out_specs, out_shape):
    nbytes = 0
    for job in casts:
        plan = _plan_cast(job, grid)
        in_specs.extend(plan.in_specs)
        args.extend(plan.args)
        out_specs.append(plan.out_spec)
        out_shape.append(plan.out_shape)
        nbytes += plan.nbytes
    return nbytes


def _rmsnorm_kernel(x_ref, g_ref, o_ref):
    o_ref[...] = _rmsnorm_rows(x_ref[...], g_ref[...]).astype(o_ref.dtype)


def rmsnorm(x, g, out_dtype, tm=512):
    m, d = x.shape
    blocks = 2 * (_nbytes((tm, d), x.dtype) + _nbytes((tm, d), out_dtype))
    return pl.pallas_call(
        _rmsnorm_kernel,
        grid=(m // tm,),
        in_specs=[pl.BlockSpec((tm, d), lambda i: (i, 0)), pl.BlockSpec((1, d), lambda i: (0, 0))],
        out_specs=pl.BlockSpec((tm, d), lambda i: (i, 0)),
        out_shape=jax.ShapeDtypeStruct((m, d), out_dtype),
        compiler_params=pltpu.CompilerParams(
            dimension_semantics=("arbitrary",), vmem_limit_bytes=_vmem_limit(blocks)),
        name="rmsnorm",
    )(x, g.reshape(1, d))


def _accumulate_row_stats(ss_ref, part, first):
    @pl.when(first)
    def _():
        ss_ref[...] = part

    @pl.when(jnp.logical_not(first))
    def _():
        ss_ref[...] += part


def _mm_kernel(*refs, relu2, residual, scaled, stats, side, gate_scale, copy_bf16, cast_gains, inv_d, w_dims,
               silu_from):
    gates = gate_scale is not None
    it = iter(refs)
    x_ref, w_ref = next(it), next(it)
    res_ref = next(it) if residual else None
    ss_in_ref = next(it) if scaled else None
    side_w_ref = next(it) if side else None
    gate_w_ref, gate_b_ref = (next(it), next(it)) if gates else (None, None)
    cast_in = _take_cast_refs(it, cast_gains)
    o_ref = next(it)
    ob_ref, ss_out_ref = (next(it), next(it)) if stats else (None, None)
    side_o_ref = next(it) if side else None
    ghi_ref, glo_ref = (next(it), next(it)) if gates else (None, None)
    copy_ref = next(it) if copy_bf16 else None
    cast_out = [next(it) for _ in cast_gains]

    j = pl.program_id(1)
    w_transposed = w_dims[0][1][0] == 1
    row_scale = lax.rsqrt(ss_in_ref[...] * inv_d + EPS) if scaled else None
    width = MXU_COLS
    tiles = o_ref.shape[1] // width

    if side:
        @pl.when(j == 0)
        def _():
            extra = lax.dot_general(x_ref[...], side_w_ref[...], w_dims, preferred_element_type=F32)
            side_o_ref[...] = extra * row_scale if scaled else extra

    def gate_block(piece, pieces):
        n_rows = side_o_ref.shape[0] // pieces
        rows = slice(piece * n_rows, (piece + 1) * n_rows)
        z = jnp.dot(side_o_ref[rows, :].astype(BF16), gate_w_ref[...], preferred_element_type=F32) + gate_b_ref[...]
        z2 = z * math.log2(math.e)
        g = (jnp.minimum(z2, 0.0) - jnp.log2(1.0 + jnp.exp2(-jnp.abs(z2)))) * gate_scale
        hi = g.astype(BF16)
        ghi_ref[rows, :] = hi
        glo_ref[rows, :] = (g - hi.astype(F32)).astype(BF16)

    def body(silu):
        ss_part = None
        for tile in range(tiles):
            cols = slice(tile * width, (tile + 1) * width)
            w_tile = w_ref[cols, :] if w_transposed else w_ref[:, cols]
            acc = lax.dot_general(x_ref[...], w_tile, w_dims, preferred_element_type=F32)
            if scaled:
                acc = acc * _lane_tile(row_scale, width)
            if relu2:
                acc = jnp.maximum(acc, 0.0)
                acc = acc * acc
            if silu:
                acc = jax.nn.silu(acc)
            if residual:
                acc = res_ref[:, cols] + acc
            o_ref[:, cols] = acc.astype(o_ref.dtype)
            if copy_bf16 and not silu:
                copy_ref[:, cols] = acc.astype(copy_ref.dtype)
            if stats:
                ob_ref[:, cols] = acc.astype(ob_ref.dtype)
                part = _row_sumsq(acc)
                ss_part = part if ss_part is None else ss_part + part
            _run_casts(cast_in, cast_out, tile, tiles)
            if gates and not silu:
                gate_block(tile, tiles)
        if stats:
            _accumulate_row_stats(ss_out_ref, ss_part, j == 0)

    if silu_from is None:
        body(False)
    else:
        in_silu = j * o_ref.shape[1] >= silu_from
        pl.when(in_silu)(functools.partial(body, True))
        pl.when(jnp.logical_not(in_silu))(functools.partial(body, False))


def matmul(x, w, out_dtype, *, n=None, w_transposed=False, relu2=False, silu_from=None, residual=None, row_ss=None,
           stats=False, side_w=None, gates=None, copy_bf16=None, casts=(), tm=1024, tn=1024, name="matmul"):
    m, k = x.shape
    n = w.shape[0 if w_transposed else 1] if n is None else n
    tn = min(tn, n)
    assert m % tm == 0 and n % tn == 0 and (silu_from is None or silu_from % tn == 0)
    grid = (m // tm, n // tn)
    row_block = pl.BlockSpec((tm, LANES), lambda i, j: (i, 0))
    tile = pl.BlockSpec((tm, tn), lambda i, j: (i, j))
    if w_transposed:
        w_spec, side_spec = pl.BlockSpec((tn, k), lambda i, j: (j, 0)), pl.BlockSpec((LANES, k), lambda i, j: (0, 0))
    else:
        w_spec, side_spec = pl.BlockSpec((k, tn), lambda i, j: (0, j)), pl.BlockSpec((k, LANES), lambda i, j: (0, 0))
    in_specs = [pl.BlockSpec((tm, k), lambda i, j: (i, 0)), w_spec]
    args = [x, w]
    blocks = _nbytes((tm, k), x.dtype) + _nbytes((k, tn), w.dtype) + _nbytes((tm, tn), out_dtype)
    if residual is not None:
        in_specs.append(tile)
        args.append(residual)
        blocks += _nbytes((tm, tn), residual.dtype)
    if row_ss is not None:
        in_specs.append(row_block)
        args.append(row_ss)
    if side_w is not None:
        in_specs.append(side_spec)
        args.append(side_w)
        blocks += _nbytes((k, LANES), side_w.dtype)
    gate_scale = None
    if gates is not None:
        gate_w, gate_b, gate_scale = gates
        ng = silu_from // tn
        gw = gate_w.shape[1] // ng
        assert side_w is not None and gate_w.shape[1] % ng == 0 and gw % LANES == 0
        gate_blk = lambda i, j: (0, jnp.minimum(j, ng - 1))
        in_specs += [pl.BlockSpec((LANES, gw), gate_blk), pl.BlockSpec((1, gw), gate_blk)]
        args += [gate_w, gate_b.reshape(1, -1)]
        blocks += _nbytes((LANES, gw), BF16) + 2 * _nbytes((tm, gw), BF16)
    out_specs = [tile]
    out_shape = [jax.ShapeDtypeStruct((m, n), out_dtype)]
    if stats:
        out_specs += [tile, row_block]
        out_shape += [jax.ShapeDtypeStruct((m, n), BF16), jax.ShapeDtypeStruct((m, LANES), F32)]
        blocks += _nbytes((tm, tn), BF16)
    if side_w is not None:
        out_specs.append(row_block)
        out_shape.append(jax.ShapeDtypeStruct((m, LANES), F32))
    if gates is not None:
        out_specs += [pl.BlockSpec((tm, gw), lambda i, j: (i, jnp.minimum(j, ng - 1)))] * 2
        out_shape += [jax.ShapeDtypeStruct((m, gate_w.shape[1]), BF16)] * 2
    if copy_bf16 is not None:
        first, width = copy_bf16
        assert first % tn == 0 and width % tn == 0 and first + width == silu_from
        lo, hi = first // tn, (first + width) // tn - 1
        out_specs.append(pl.BlockSpec((tm, tn), lambda i, j: (i, jnp.clip(j, lo, hi) - lo)))
        out_shape.append(jax.ShapeDtypeStruct((m, width), BF16))
        blocks += _nbytes((tm, tn), BF16)
    blocks += _add_casts(casts, grid, in_specs, args, out_specs, out_shape)
    return pl.pallas_call(
        functools.partial(_mm_kernel, relu2=relu2, residual=residual is not None, scaled=row_ss is not None,
                          stats=stats, side=side_w is not None, gate_scale=gate_scale,
                          copy_bf16=copy_bf16 is not None,
                          cast_gains=tuple(job.gain is not None for job in casts), inv_d=1.0 / k,
                          w_dims=(((1,), (1 if w_transposed else 0,)), ((), ())), silu_from=silu_from),
        grid=grid,
        in_specs=in_specs,
        out_specs=out_specs,
        out_shape=out_shape,
        compiler_params=pltpu.CompilerParams(
            dimension_semantics=("arbitrary", "arbitrary"),
            vmem_limit_bytes=_vmem_limit(2 * blocks + _nbytes((tm, tn), F32))),
        name=name,
    )(*args)


def _mm_ktiled_kernel(x_ref, w_ref, res_ref, o_ref, *stat_refs, nk):
    j, kk = pl.program_id(1), pl.program_id(2)
    col_tiles = [slice(c0, c0 + MXU_COLS) for c0 in range(0, o_ref.shape[1], MXU_COLS)]

    def step(first, final):
        ss_part = None
        for cols in col_tiles:
            base = res_ref[:, cols] if first else o_ref[:, cols]
            out = base + jnp.dot(x_ref[...], w_ref[:, cols], preferred_element_type=F32)
            o_ref[:, cols] = out
            if final and stat_refs:
                stat_refs[0][:, cols] = out.astype(stat_refs[0].dtype)
                part = _row_sumsq(out)
                ss_part = part if ss_part is None else ss_part + part
        if final and stat_refs:
            _accumulate_row_stats(stat_refs[1], ss_part, j == 0)

    if nk == 1:
        step(True, True)
        return
    pl.when(kk == 0)(functools.partial(step, True, False))
    if stat_refs:
        if nk > 2:
            pl.when((kk > 0) & (kk < nk - 1))(functools.partial(step, False, False))
        pl.when(kk == nk - 1)(functools.partial(step, False, True))
    else:
        pl.when(kk > 0)(functools.partial(step, False, False))


def matmul_ktiled(x, w, residual, *, stats=False, tm=1024, tn=1024, tk=4096, name="matmul_ktiled"):
    m, k = x.shape
    n = w.shape[1]
    assert m % tm == 0 and n % tn == 0 and k % tk == 0
    tile = pl.BlockSpec((tm, tn), lambda i, j, kk: (i, j))
    out_specs = [tile]
    out_shape = [jax.ShapeDtypeStruct((m, n), F32)]
    blocks = (_nbytes((tm, tk), x.dtype) + _nbytes((tk, tn), w.dtype) + 2 * _nbytes((tm, tn), F32))
    if stats:
        out_specs += [tile, pl.BlockSpec((tm, LANES), lambda i, j, kk: (i, 0))]
        out_shape += [jax.ShapeDtypeStruct((m, n), BF16), jax.ShapeDtypeStruct((m, LANES), F32)]
        blocks += _nbytes((tm, tn), BF16)
    return pl.pallas_call(
        functools.partial(_mm_ktiled_kernel, nk=k // tk),
        grid=(m // tm, n // tn, k // tk),
        in_specs=[
            pl.BlockSpec((tm, tk), lambda i, j, kk: (i, kk)),
            pl.BlockSpec((tk, tn), lambda i, j, kk: (kk, j)),
            tile,
        ],
        out_specs=out_specs,
        out_shape=out_shape,
        compiler_params=pltpu.CompilerParams(
            dimension_semantics=("arbitrary", "arbitrary", "arbitrary"),
            vmem_limit_bytes=_vmem_limit(2 * blocks + _nbytes((tm, tn), F32))),
        name=name,
    )(x, w, residual)


def _pool_kernel(x_hbm, xp_ref, xn_ref, gmix_ref, gmlp_ref, w_ref, scale_ref, wsrc_hbm, h_ref, hn_ref, wdst_ref,
                 xring, wring, xsem, wsem, buf_ref, lvl_ref, *, tm, seq, halo, first_slab_row):
    i = pl.program_id(1)
    nblk = pl.num_programs(1)
    step = pl.program_id(0) * nblk + i
    steps = pl.num_programs(0) * nblk
    slab = wring.shape[1]

    def copies(t):
        slot = t % RING_SLOTS
        return (pltpu.make_async_copy(x_hbm.at[pl.ds(pl.multiple_of(t * tm, tm), tm)], xring.at[slot],
                                      xsem.at[slot]),
                pltpu.make_async_copy(wsrc_hbm.at[pl.ds(pl.multiple_of(first_slab_row + t * slab, slab), slab)],
                                      wring.at[slot], wsem.at[slot]))

    def start(t):
        for copy in copies(t):
            copy.start()

    @pl.when(step == 0)
    def _():
        for t in range(RING_SLOTS - 1):
            start(t)

    ahead = step + (RING_SLOTS - 1)
    pl.when(ahead < steps)(functools.partial(start, ahead))
    for copy in copies(step):
        copy.wait()
    ring_slot = step % RING_SLOTS

    gmix = gmix_ref[...]
    x = xring[ring_slot]
    d_model = x.shape[-1]
    group = d_model // len(POOL_WINDOWS)

    ext = tm + 2 * halo
    buf_ref[halo:halo + tm, :] = _rmsnorm_rows(x, gmix)
    buf_ref[0:halo, :] = jnp.where(i > 0, _rmsnorm_rows(xp_ref[0], gmix), 0.0)
    buf_ref[halo + tm:ext, :] = jnp.where(i < nblk - 1, _rmsnorm_rows(xn_ref[0], gmix), 0.0)
    buf_ref[ext:ext + halo, :] = jnp.zeros((halo, d_model), F32)
    lvl_ref[:, ext:ext + halo, :] = jnp.zeros((2, halo, group), F32)

    pos = i * tm + lax.broadcasted_iota(jnp.int32, (tm, 1), 0)
    for gi, win in enumerate(POOL_WINDOWS):
        cols = slice(gi * group, (gi + 1) * group)
        half = win // 2
        load = lambda lo, n: buf_ref[lo:lo + n, cols]
        span, slot = 1, 0
        while span < half:
            lvl_ref[slot, 0:ext, :] = load(0, ext) + load(span, ext)
            load = functools.partial(lambda s, lo, n: lvl_ref[s, lo:lo + n, :], slot)
            span, slot = 2 * span, 1 - slot
        acc = load(halo - half, tm) + load(halo, tm)
        count = (jnp.minimum(pos + half, seq) - jnp.maximum(pos - half, 0)).astype(F32)
        diff = acc * (1.0 / count) - buf_ref[halo:halo + tm, cols]
        y = jnp.dot(diff.astype(BF16), w_ref[gi], preferred_element_type=F32)
        h_ref[0, :, cols] = x[:, cols] + y * scale_ref[:, cols]

    hn_ref[0] = _rmsnorm_rows(h_ref[0], gmlp_ref[...]).astype(hn_ref.dtype)
    wdst_ref[...] = wring[ring_slot].astype(wdst_ref.dtype)


def pool_layer(x, g_mix, g_mlp, pool_w, pool_scale, cast, tm=256):
    b, s, d = x.shape
    halo = SUBLANES
    assert max(POOL_WINDOWS) // 2 <= halo and tm % halo == 0 and s % tm == 0
    ng, gd, _ = pool_w.shape
    hb = tm // halo
    grid = (b, s // tm)
    steps = b * (s // tm)
    _, r, c = cast.src.shape
    assert cast.gain is None and r % (steps * 2 * SUBLANES) == 0 and steps >= RING_SLOTS - 1
    slab = r // steps
    row_block = pl.BlockSpec((1, tm, d), lambda bi, i: (bi, i, 0))
    vec = pl.BlockSpec((1, d), lambda bi, i: (0, 0))
    streamed = pl.BlockSpec(memory_space=pl.ANY)
    in_specs = [
        streamed,
        pl.BlockSpec((1, halo, d), lambda bi, i: (bi, jnp.maximum(i * hb - 1, 0), 0)),
        pl.BlockSpec((1, halo, d), lambda bi, i: (bi, jnp.minimum((i + 1) * hb, s // halo - 1), 0)),
        vec,
        vec,
        pl.BlockSpec((ng, gd, gd), lambda bi, i: (0, 0, 0), pipeline_mode=pl.Buffered(1)),
        vec,
        streamed,
    ]
    args = [x.reshape(b * s, d), x, x, g_mix.reshape(1, d), g_mlp.reshape(1, d), pool_w.astype(BF16),
            pool_scale.reshape(1, d), cast.src.reshape(-1, c)]
    out_specs = [row_block, row_block, pl.BlockSpec((slab, c), lambda bi, i: (bi * (s // tm) + i, 0))]
    out_shape = [jax.ShapeDtypeStruct((b, s, d), F32), jax.ShapeDtypeStruct((b, s, d), BF16),
                 jax.ShapeDtypeStruct((r, c), BF16)]
    blocks = (2 * (_nbytes((tm, d), F32) + _nbytes((tm, d), BF16) + _nbytes((slab, c), BF16))
              + _nbytes(pool_w.shape, BF16))
    scratch = [pltpu.VMEM((RING_SLOTS, tm, d), F32), pltpu.VMEM((RING_SLOTS, slab, c), F32),
               pltpu.SemaphoreType.DMA((RING_SLOTS,)), pltpu.SemaphoreType.DMA((RING_SLOTS,)),
               pltpu.VMEM((tm + 3 * halo, d), F32), pltpu.VMEM((2, tm + 3 * halo, gd), F32)]
    scratch_bytes = (RING_SLOTS * (_nbytes((tm, d), F32) + _nbytes((slab, c), F32))
                     + _nbytes((tm + 3 * halo, d + 2 * gd), F32))
    return pl.pallas_call(
        functools.partial(_pool_kernel, tm=tm, seq=s, halo=halo, first_slab_row=cast.layer * r),
        grid=grid,
        in_specs=in_specs,
        out_specs=out_specs,
        out_shape=out_shape,
        scratch_shapes=scratch,
        compiler_params=pltpu.CompilerParams(
            dimension_semantics=("arbitrary", "arbitrary"),
            vmem_limit_bytes=_vmem_limit(blocks + scratch_bytes)),
        name="pool_layer",
    )(*args)


def _gla_block_stages(q_ref, k_ref, v_ref, ghi_ref, glo_ref, state_ref, finish, *, head, row0, rows, dk, dv,
                       reverse):
    c = CHUNK
    ns = rows // c
    order = tuple(reversed(range(ns))) if reverse else tuple(range(ns))
    kcols = slice(head * dk, (head + 1) * dk)
    vcols = slice(head * dv, (head + 1) * dv)

    def load(ref, cols):
        return jnp.concatenate([ref[0, row0 + ci * c:row0 + (ci + 1) * c, cols] for ci in order], axis=0)

    q = load(q_ref, kcols) * (dk ** -0.5)
    k = load(k_ref, kcols)

    row = lax.broadcasted_iota(jnp.int32, (rows, rows), 0)
    col = lax.broadcasted_iota(jnp.int32, (rows, rows), 1)
    shift = c.bit_length() - 1
    row_chunk = lax.shift_right_logical(row, shift)
    col_chunk = lax.shift_right_logical(col, shift)
    causal = (col >= row) if reverse else (col <= row)
    same_chunk = (row_chunk == col_chunk) & causal
    tri = jnp.where(same_chunk, 1.0, 0.0).astype(BF16)
    b = (jnp.dot(tri, load(ghi_ref, kcols), preferred_element_type=F32)
         + jnp.dot(tri, load(glo_ref, kcols), preferred_element_type=F32))

    last_i, mid_i = (0, c - c // 2) if reverse else (c - 1, c // 2 - 1)
    b_last = [b[p * c + last_i:p * c + last_i + 1, :] for p in range(ns)]
    b_mid = [b[p * c + mid_i:p * c + mid_i + 1, :] for p in range(ns)]
    a = [jnp.exp2(x) for x in b_last]
    yield False

    def prod(factors):
        out = None
        for f in factors:
            out = f if out is None else out * f
        return out

    q_intra, k_intra, q_inter, k_state, q_prev, k_next = [], [], [], [], [], []
    q_skip = {d: [] for d in range(2, ns)}
    for p in range(ns):
        rs = slice(p * c, (p + 1) * c)
        bp, qp, kp = b[rs], q[rs], k[rs]
        qn = qp * jnp.exp2(bp)
        q_intra.append((qp * jnp.exp2(bp - b_mid[p])).astype(BF16))
        q_inter.append(qn.astype(BF16))
        before = prod(a[:p])
        q_prev.append((qn if before is None else qn * before).astype(BF16))
        for d in range(2, p + 1):
            q_skip[d].append((qn * prod(a[p - d + 1:p])).astype(BF16))
        yield False
        ks = kp * jnp.exp2(b_last[p] - bp)
        k_intra.append((kp * jnp.exp2(b_mid[p] - bp)).astype(BF16))
        k_state.append(ks.astype(BF16))
        after = prod(a[p + 1:])
        k_next.append((ks if after is None else ks * after).astype(BF16))
        yield False
    v = load(v_ref, vcols)
    yield True

    nt = (((1,), (1,)), ((), ()))
    s_intra = lax.dot_general(jnp.concatenate(q_intra, axis=0), jnp.concatenate(k_intra, axis=0), nt,
                              preferred_element_type=F32)
    yield None
    cross_lhs = q_inter + [x for d in range(2, ns) for x in q_skip[d]]
    s_cross = lax.dot_general(jnp.concatenate(cross_lhs, axis=0), jnp.concatenate(k_state, axis=0), nt,
                              preferred_element_type=F32)
    yield None

    skip_base = {}
    base = ns * c
    for d in range(2, ns):
        skip_base[d] = base
        base += (ns - d) * c
    score_rows = []
    row_in = lax.broadcasted_iota(jnp.int32, (c, rows), 0)
    col_in = lax.broadcasted_iota(jnp.int32, (c, rows), 1)
    col_chunk_in = lax.shift_right_logical(col_in, shift)
    for p in range(ns):
        rs = slice(p * c, (p + 1) * c)
        col_local = col_in - p * c
        causal_in = (col_local >= row_in) if reverse else (col_local <= row_in)
        sc = jnp.where((col_chunk_in == p) & causal_in, s_intra[rs], 0.0)
        if p >= 1:
            sc = jnp.where(col_chunk_in == p - 1, s_cross[rs], sc)
        for d in range(2, p + 1):
            off = skip_base[d] + (p - d) * c
            sc = jnp.where(col_chunk_in == p - d, s_cross[off:off + c], sc)
        score_rows.append(sc.astype(BF16))
    scores = jnp.concatenate(score_rows, axis=0)
    o = jnp.dot(scores, v, preferred_element_type=F32)
    yield None

    state = state_ref[head]
    o = o + jnp.dot(jnp.concatenate(q_prev, axis=0), state.astype(BF16), preferred_element_type=F32)
    yield None

    upd = lax.dot_general(jnp.concatenate(k_next, axis=0), v, (((0,), (0,)), ((), ())),
                          preferred_element_type=F32)
    decay_t = jnp.transpose(jnp.broadcast_to(prod(a), (LANES, dk)))
    state_ref[head] = _lane_tile(decay_t, dv) * state + upd
    yield None

    for p, ci in enumerate(order):
        finish(row0 + ci * c, o[p * c:(p + 1) * c])
        yield None


def _run_staggered(blocks):
    prev = None
    for gen in blocks:
        ready = False
        while not ready:
            ready = next(gen)
            if prev is not None:
                next(prev, None)
        if prev is not None:
            for _ in prev:
                pass
        prev = gen
    for _ in prev:
        pass


def _gla_kernel(*refs, nb, heads, sub, dk, dv, cast_gains):
    it = iter(refs)
    q_ref, k_ref, v_ref, ghi_ref, glo_ref, gate_ref = (next(it) for _ in range(6))
    cast_in = _take_cast_refs(it, cast_gains)
    o_ref = next(it)
    cast_out = [next(it) for _ in cast_gains]
    state_ref, acc_ref = next(it), next(it)

    s = pl.program_id(2)
    rows = q_ref.shape[1]
    stages = functools.partial(_gla_block_stages, q_ref, k_ref, v_ref, ghi_ref, glo_ref, state_ref,
                               rows=sub, dk=dk, dv=dv)
    starts = tuple(range(0, rows, sub))

    @pl.when((s == 0) | (s == nb))
    def _():
        state_ref[...] = jnp.zeros_like(state_ref)

    @pl.when(s < nb)
    def _():
        base = pl.multiple_of(s * rows, rows)

        def keep(head, row, piece):
            acc_ref[pl.ds(base + row, CHUNK), head * dv:(head + 1) * dv] = piece

        _run_casts(cast_in, cast_out)
        _run_staggered([stages(functools.partial(keep, head), head=head, row0=row0, reverse=False)
                        for head in range(heads) for row0 in starts])

    @pl.when(s >= nb)
    def _():
        base = pl.multiple_of((2 * nb - 1 - s) * rows, rows)

        def emit(head, row, piece):
            vcols = slice(head * dv, (head + 1) * dv)
            o = acc_ref[pl.ds(base + row, CHUNK), vcols] + piece
            o = o * lax.rsqrt(jnp.mean(o * o, axis=-1, keepdims=True) + EPS)
            o_ref[0, row:row + CHUNK, vcols] = (o * gate_ref[0, row:row + CHUNK, vcols]).astype(o_ref.dtype)

        _run_casts(cast_in, cast_out)
        _run_staggered([stages(functools.partial(emit, head), head=head, row0=row0, reverse=True)
                        for head in range(heads) for row0 in reversed(starts)])


def gla_core(p, v, g_hi, g_lo, *, heads, casts=(), bs=1024, sub=256, heads_per_step=1):
    b, s, width = p.shape
    dv_total = width // 3
    dk_total = dv_total // 2
    dk, dv = dk_total // heads, dv_total // heads
    hp = heads_per_step
    bs = min(bs, s)
    assert heads % hp == 0 and s % bs == 0 and bs % sub == 0 and sub % CHUNK == 0
    gk, gv = hp * dk, hp * dv
    nb = s // bs
    groups = heads // hp
    kq = dk_total // gk
    kg = (2 * dk_total + dv_total) // gv
    grid = (b, groups, 2 * nb)

    def blk(i):
        return jnp.where(i < nb, i, 2 * nb - 1 - i)

    def out_blk(i):
        return jnp.where(i < nb, nb - 1, 2 * nb - 1 - i)

    decay_spec = pl.BlockSpec((1, bs, gk), lambda bi, h, i: (bi, blk(i), (i // nb) * groups + h))
    in_specs = [
        pl.BlockSpec((1, bs, gk), lambda bi, h, i: (bi, blk(i), h)),
        pl.BlockSpec((1, bs, gk), lambda bi, h, i: (bi, blk(i), kq + h)),
        pl.BlockSpec((1, bs, gv), lambda bi, h, i: (bi, blk(i), h)),
        decay_spec,
        decay_spec,
        pl.BlockSpec((1, bs, gv), lambda bi, h, i: (bi, out_blk(i), kg + h)),
    ]
    args = [p, p, v, g_hi, g_lo, p]
    out_specs = [pl.BlockSpec((1, bs, gv), lambda bi, h, i: (bi, out_blk(i), h))]
    out_shape = [jax.ShapeDtypeStruct((b, s, dv_total), BF16)]
    blocks = (2 * _nbytes((bs, gk), F32) + _nbytes((bs, gv), F32) + 2 * _nbytes((bs, gk), BF16)
              + 2 * _nbytes((bs, gv), BF16))
    blocks += _add_casts(casts, grid, in_specs, args, out_specs, out_shape)
    scratch_bytes = _nbytes((hp, dk, dv), F32) + _nbytes((s, gv), F32)
    return pl.pallas_call(
        functools.partial(_gla_kernel, nb=nb, heads=hp, sub=sub, dk=dk, dv=dv,
                          cast_gains=tuple(job.gain is not None for job in casts)),
        grid=grid,
        in_specs=in_specs,
        out_specs=out_specs,
        out_shape=out_shape,
        scratch_shapes=[pltpu.VMEM((hp, dk, dv), F32), pltpu.VMEM((s, gv), F32)],
        compiler_params=pltpu.CompilerParams(
            dimension_semantics=("arbitrary", "arbitrary", "arbitrary"),
            vmem_limit_bytes=_vmem_limit(2 * blocks + scratch_bytes)),
        name="gla_scan",
    )(*args)


def gla_layer(h, hb, h_ss, w_in_t, w_out_f32, layer, w_up_f, b_up_f, w_up_b, b_up_b, g_norm, next_cast):
    b, s, d = h.shape
    t = b * s
    main = w_in_t.shape[0] - 2 * GATE_RANK
    h2 = h.reshape(t, d)
    w_r_t = jnp.pad(w_in_t[main:], ((0, LANES - 2 * GATE_RANK), (0, 0)))
    gate_w = jnp.concatenate([jnp.pad(w_up_f, ((0, LANES - GATE_RANK), (0, 0))),
                              jnp.pad(w_up_b, ((GATE_RANK, LANES - 2 * GATE_RANK), (0, 0)))], axis=1).astype(BF16)
    gate_b = jnp.concatenate([b_up_f, b_up_b])
    dv_total = w_out_f32.shape[1]
    dk2 = gate_w.shape[1]
    p, _, g_hi, g_lo, v, w_out, next_bf16 = matmul(
        hb, w_in_t, F32, n=main, w_transposed=True, silu_from=main - dv_total, row_ss=h_ss, side_w=w_r_t,
        gates=(gate_w, gate_b, 1.0 / GATE_TAU), copy_bf16=(dk2, dv_total),
        casts=(CastJob(w_out_f32, layer, gain=jnp.tile(g_norm, GLA_HEADS)), next_cast), name="gla_in_proj")
    og = gla_core(p.reshape(b, s, main), v.reshape(b, s, dv_total), g_hi.reshape(b, s, dk2),
                  g_lo.reshape(b, s, dk2), heads=GLA_HEADS)[0]
    out, out_b, out_ss = matmul(og.reshape(t, -1), w_out, F32, residual=h2, stats=True, name="gla_out_proj")
    return out, out_b, out_ss, next_bf16


def kernel(x, norm_mix, norm_mlp, norm_final, pool_w, pool_scale, gla_w_in, gla_w_up_f, gla_b_up_f, gla_w_up_b,
           gla_b_up_b, gla_g_norm, gla_w_out, mlp_w_in, mlp_w_out):
    b, s, d = x.shape
    t = b * s
    w_in_t_f32 = jnp.swapaxes(gla_w_in, 1, 2)
    h, hn, w1 = pool_layer(x, norm_mix[0], norm_mlp[0], pool_w[0], pool_scale[0], CastJob(mlp_w_in, 0))
    a, w2, w_in = matmul(hn.reshape(t, d), w1, BF16, relu2=True,
                         casts=(CastJob(mlp_w_out, 0), CastJob(w_in_t_f32, 0, gain=norm_mix[1], gain_axis=1)),
                         name="mlp_up_proj")
    h, hb, h_ss = matmul_ktiled(a, w2, h.reshape(t, d), stats=True, name="mlp_down_proj")

    h, hb, h_ss, w1 = gla_layer(h.reshape(b, s, d), hb, h_ss, w_in, gla_w_out, 0, gla_w_up_f[0], gla_b_up_f[0],
                                gla_w_up_b[0], gla_b_up_b[0], gla_g_norm[0],
                                CastJob(mlp_w_in, 1, gain=norm_mlp[1]))

    a, w2 = matmul(hb, w1, BF16, relu2=True, row_ss=h_ss, casts=(CastJob(mlp_w_out, 1),), name="mlp_up_proj")
    (h,) = matmul_ktiled(a, w2, h, name="mlp_down_proj")
    return rmsnorm(h, norm_final, F32).reshape(b, s, d)
```

```python
import functools
import math
from typing import NamedTuple, Optional

import jax
import jax.numpy as jnp
from jax import lax
from jax.experimental import pallas as pl
from jax.experimental.pallas import tpu as pltpu

EPS = 1e-6
POOL_WINDOWS = (2, 4, 8, 16)
GLA_HEADS = 4
GATE_RANK = 16
GATE_TAU = 16.0
CHUNK = 64

V7X_VMEM_BYTES = 64 * 1024 * 1024
VMEM_UNSCOPED_BYTES = 4 * 1024 * 1024
LANES = 128
SUBLANES = 8
MXU_COLS = 256

F32 = jnp.float32
BF16 = jnp.bfloat16


def _vmem_limit(block_bytes):
    return int(min(V7X_VMEM_BYTES - VMEM_UNSCOPED_BYTES, 2 * block_bytes))


def _nbytes(shape, dtype):
    n = 1
    for s in shape:
        n *= s
    return n * jnp.dtype(dtype).itemsize


def _lane_tile(x, n):
    return jnp.concatenate([x] * (n // LANES), axis=1)


def _row_sumsq(x):
    return jnp.broadcast_to(jnp.sum(x * x, axis=-1, keepdims=True), (x.shape[0], LANES))


def _rmsnorm_rows(x, g):
    y = x * lax.rsqrt(jnp.mean(x * x, axis=-1, keepdims=True) + EPS)
    return y * g


class CastJob(NamedTuple):
    src: jax.Array
    layer: int
    gain: Optional[jax.Array] = None
    gain_axis: int = 0


class _CastPlan(NamedTuple):
    in_specs: list
    args: list
    out_spec: pl.BlockSpec
    out_shape: jax.ShapeDtypeStruct
    nbytes: int


def _plan_cast(job, grid):
    _, r, c = job.src.shape
    steps = 1
    for g in grid:
        steps *= g
    bf16_rows = 2 * SUBLANES
    nslabs = next(n for n in range(min(steps, r // bf16_rows), 0, -1) if r % (n * bf16_rows) == 0)
    slab = r // nslabs

    def slab_of(*idx):
        step = 0
        for g, i in zip(grid, idx):
            step = step * g + i
        return step * nslabs // steps

    layer = job.layer
    in_specs = [pl.BlockSpec((None, slab, c), lambda *idx: (layer, slab_of(*idx), 0))]
    args = [job.src]
    if job.gain is not None and job.gain_axis == 0:
        in_specs.append(pl.BlockSpec((slab, 1), lambda *idx: (slab_of(*idx), 0)))
        args.append(job.gain.reshape(r, 1))
    elif job.gain is not None:
        in_specs.append(pl.BlockSpec((1, c), lambda *idx: (0, 0)))
        args.append(job.gain.reshape(1, c))
    return _CastPlan(in_specs, args, pl.BlockSpec((slab, c), lambda *idx: (slab_of(*idx), 0)),
                     jax.ShapeDtypeStruct((r, c), BF16), _nbytes((slab, c), F32) + _nbytes((slab, c), BF16))


def _take_cast_refs(it, gains):
    return [(next(it), next(it) if g else None) for g in gains]


def _run_casts(cast_in, cast_out, piece=0, pieces=1):
    for (src, gain), dst in zip(cast_in, cast_out):
        width = dst.shape[1] // pieces
        assert width % LANES == 0
        cols = slice(piece * width, (piece + 1) * width)
        val = src[:, cols]
        if gain is not None:
            val = val * (gain[:, cols] if gain.shape[1] == dst.shape[1] else gain[...])
        dst[:, cols] = val.astype(dst.dtype)


def _add_casts(casts, grid, in_specs, args, out_specs, out_shape):
    nbytes = 0
    for job in casts:
        plan = _plan_cast(job, grid)
        in_specs.extend(plan.in_specs)
        args.extend(plan.args)
        out_specs.append(plan.out_spec)
        out_shape.append(plan.out_shape)
        nbytes += plan.nbytes
    return nbytes


def _rmsnorm_kernel(x_ref, g_ref, o_ref):
    o_ref[...] = _rmsnorm_rows(x_ref[...], g_ref[...]).astype(o_ref.dtype)


def rmsnorm(x, g, out_dtype, tm=512):
    m, d = x.shape
    blocks = 2 * (_nbytes((tm, d), x.dtype) + _nbytes((tm, d), out_dtype))
    return pl.pallas_call(
        _rmsnorm_kernel,
        grid=(m // tm,),
        in_specs=[pl.BlockSpec((tm, d), lambda i: (i, 0)), pl.BlockSpec((1, d), lambda i: (0, 0))],
        out_specs=pl.BlockSpec((tm, d), lambda i: (i, 0)),
        out_shape=jax.ShapeDtypeStruct((m, d), out_dtype),
        compiler_params=pltpu.CompilerParams(
            dimension_semantics=("arbitrary",), vmem_limit_bytes=_vmem_limit(blocks)),
        name="rmsnorm",
    )(x, g.reshape(1, d))


def _accumulate_row_stats(ss_ref, part, first):
    @pl.when(first)
    def _():
        ss_ref[...] = part

    @pl.when(jnp.logical_not(first))
    def _():
        ss_ref[...] += part


def _mm_kernel(*refs, relu2, residual, scaled, stats, side, gate_scale, copy_bf16, cast_gains, inv_d, w_dims,
               silu_from):
    gates = gate_scale is not None
    it = iter(refs)
    x_ref, w_ref = next(it), next(it)
    res_ref = next(it) if residual else None
    ss_in_ref = next(it) if scaled else None
    side_w_ref = next(it) if side else None
    gate_w_ref, gate_b_ref = (next(it), next(it)) if gates else (None, None)
    cast_in = _take_cast_refs(it, cast_gains)
    o_ref = next(it)
    ob_ref, ss_out_ref = (next(it), next(it)) if stats else (None, None)
    side_o_ref = next(it) if side else None
    ghi_ref, glo_ref = (next(it), next(it)) if gates else (None, None)
    copy_ref = next(it) if copy_bf16 else None
    cast_out = [next(it) for _ in cast_gains]

    j = pl.program_id(1)
    w_transposed = w_dims[0][1][0] == 1
    row_scale = lax.rsqrt(ss_in_ref[...] * inv_d + EPS) if scaled else None
    width = MXU_COLS
    tiles = o_ref.shape[1] // width

    if side:
        @pl.when(j == 0)
        def _():
            extra = lax.dot_general(x_ref[...], side_w_ref[...], w_dims, preferred_element_type=F32)
            side_o_ref[...] = extra * row_scale if scaled else extra

    def gate_block(piece, pieces):
        n_rows = side_o_ref.shape[0] // pieces
        rows = slice(piece * n_rows, (piece + 1) * n_rows)
        z = jnp.dot(side_o_ref[rows, :].astype(BF16), gate_w_ref[...], preferred_element_type=F32) + gate_b_ref[...]
        z2 = z * math.log2(math.e)
        g = (jnp.minimum(z2, 0.0) - jnp.log2(1.0 + jnp.exp2(-jnp.abs(z2)))) * gate_scale
        hi = g.astype(BF16)
        ghi_ref[rows, :] = hi
        glo_ref[rows, :] = (g - hi.astype(F32)).astype(BF16)

    def body(silu):
        ss_part = None
        for tile in range(tiles):
            cols = slice(tile * width, (tile + 1) * width)
            w_tile = w_ref[cols, :] if w_transposed else w_ref[:, cols]
            acc = lax.dot_general(x_ref[...], w_tile, w_dims, preferred_element_type=F32)
            if scaled:
                acc = acc * _lane_tile(row_scale, width)
            if relu2:
                acc = jnp.maximum(acc, 0.0)
                acc = acc * acc
            if silu:
                acc = jax.nn.silu(acc)
            if residual:
                acc = res_ref[:, cols] + acc
            o_ref[:, cols] = acc.astype(o_ref.dtype)
            if copy_bf16 and not silu:
                copy_ref[:, cols] = acc.astype(copy_ref.dtype)
            if stats:
                ob_ref[:, cols] = acc.astype(ob_ref.dtype)
                part = _row_sumsq(acc)
                ss_part = part if ss_part is None else ss_part + part
            _run_casts(cast_in, cast_out, tile, tiles)
            if gates and not silu:
                gate_block(tile, tiles)
        if stats:
            _accumulate_row_stats(ss_out_ref, ss_part, j == 0)

    if silu_from is None:
        body(False)
    else:
        in_silu = j * o_ref.shape[1] >= silu_from
        pl.when(in_silu)(functools.partial(body, True))
        pl.when(jnp.logical_not(in_silu))(functools.partial(body, False))


def matmul(x, w, out_dtype, *, n=None, w_transposed=False, relu2=False, silu_from=None, residual=None, row_ss=None,
           stats=False, side_w=None, gates=None, copy_bf16=None, casts=(), tm=1024, tn=1024, name="matmul"):
    m, k = x.shape
    n = w.shape[0 if w_transposed else 1] if n is None else n
    tn = min(tn, n)
    assert m % tm == 0 and n % tn == 0 and (silu_from is None or silu_from % tn == 0)
    grid = (m // tm, n // tn)
    row_block = pl.BlockSpec((tm, LANES), lambda i, j: (i, 0))
    tile = pl.BlockSpec((tm, tn), lambda i, j: (i, j))
    if w_transposed:
        w_spec, side_spec = pl.BlockSpec((tn, k), lambda i, j: (j, 0)), pl.BlockSpec((LANES, k), lambda i, j: (0, 0))
    else:
        w_spec, side_spec = pl.BlockSpec((k, tn), lambda i, j: (0, j)), pl.BlockSpec((k, LANES), lambda i, j: (0, 0))
    in_specs = [pl.BlockSpec((tm, k), lambda i, j: (i, 0)), w_spec]
    args = [x, w]
    blocks = _nbytes((tm, k), x.dtype) + _nbytes((k, tn), w.dtype) + _nbytes((tm, tn), out_dtype)
    if residual is not None:
        in_specs.append(tile)
        args.append(residual)
        blocks += _nbytes((tm, tn), residual.dtype)
    if row_ss is not None:
        in_specs.append(row_block)
        args.append(row_ss)
    if side_w is not None:
        in_specs.append(side_spec)
        args.append(side_w)
        blocks += _nbytes((k, LANES), side_w.dtype)
    gate_scale = None
    if gates is not None:
        gate_w, gate_b, gate_scale = gates
        ng = silu_from // tn
        gw = gate_w.shape[1] // ng
        assert side_w is not None and gate_w.shape[1] % ng == 0 and gw % LANES == 0
        gate_blk = lambda i, j: (0, jnp.minimum(j, ng - 1))
        in_specs += [pl.BlockSpec((LANES, gw), gate_blk), pl.BlockSpec((1, gw), gate_blk)]
        args += [gate_w, gate_b.reshape(1, -1)]
        blocks += _nbytes((LANES, gw), BF16) + 2 * _nbytes((tm, gw), BF16)
    out_specs = [tile]
    out_shape = [jax.ShapeDtypeStruct((m, n), out_dtype)]
    if stats:
        out_specs += [tile, row_block]
        out_shape += [jax.ShapeDtypeStruct((m, n), BF16), jax.ShapeDtypeStruct((m, LANES), F32)]
        blocks += _nbytes((tm, tn), BF16)
    if side_w is not None:
        out_specs.append(row_block)
        out_shape.append(jax.ShapeDtypeStruct((m, LANES), F32))
    if gates is not None:
        out_specs += [pl.BlockSpec((tm, gw), lambda i, j: (i, jnp.minimum(j, ng - 1)))] * 2
        out_shape += [jax.ShapeDtypeStruct((m, gate_w.shape[1]), BF16)] * 2
    if copy_bf16 is not None:
        first, width = copy_bf16
        assert first % tn == 0 and width % tn == 0 and first + width == silu_from
        lo, hi = first // tn, (first + width) // tn - 1
        out_specs.append(pl.BlockSpec((tm, tn), lambda i, j: (i, jnp.clip(j, lo, hi) - lo)))
        out_shape.append(jax.ShapeDtypeStruct((m, width), BF16))
        blocks += _nbytes((tm, tn), BF16)
    blocks += _add_casts(casts, grid, in_specs, args, out_specs, out_shape)
    return pl.pallas_call(
        functools.partial(_mm_kernel, relu2=relu2, residual=residual is not None, scaled=row_ss is not None,
                          stats=stats, side=side_w is not None, gate_scale=gate_scale,
                          copy_bf16=copy_bf16 is not None,
                          cast_gains=tuple(job.gain is not None for job in casts), inv_d=1.0 / k,
                          w_dims=(((1,), (1 if w_transposed else 0,)), ((), ())), silu_from=silu_from),
        grid=grid,
        in_specs=in_specs,
        out_specs=out_specs,
        out_shape=out_shape,
        compiler_params=pltpu.CompilerParams(
            dimension_semantics=("arbitrary", "arbitrary"),
            vmem_limit_bytes=_vmem_limit(2 * blocks + _nbytes((tm, tn), F32))),
        name=name,
    )(*args)


def _mm_ktiled_kernel(x_ref, w_ref, res_ref, o_ref, *stat_refs, nk):
    j, kk = pl.program_id(1), pl.program_id(2)
    col_tiles = [slice(c0, c0 + MXU_COLS) for c0 in range(0, o_ref.shape[1], MXU_COLS)]

    def step(first, final):
        ss_part = None
        for cols in col_tiles:
            base = res_ref[:, cols] if first else o_ref[:, cols]
            out = base + jnp.dot(x_ref[...], w_ref[:, cols], preferred_element_type=F32)
            o_ref[:, cols] = out
            if final and stat_refs:
                stat_refs[0][:, cols] = out.astype(stat_refs[0].dtype)
                part = _row_sumsq(out)
                ss_part = part if ss_part is None else ss_part + part
        if final and stat_refs:
            _accumulate_row_stats(stat_refs[1], ss_part, j == 0)

    if nk == 1:
        step(True, True)
        return
    pl.when(kk == 0)(functools.partial(step, True, False))
    if stat_refs:
        if nk > 2:
            pl.when((kk > 0) & (kk < nk - 1))(functools.partial(step, False, False))
        pl.when(kk == nk - 1)(functools.partial(step, False, True))
    else:
        pl.when(kk > 0)(functools.partial(step, False, False))


def matmul_ktiled(x, w, residual, *, stats=False, tm=1024, tn=1024, tk=4096, name="matmul_ktiled"):
    m, k = x.shape
    n = w.shape[1]
    assert m % tm == 0 and n % tn == 0 and k % tk == 0
    tile = pl.BlockSpec((tm, tn), lambda i, j, kk: (i, j))
    out_specs = [tile]
    out_shape = [jax.ShapeDtypeStruct((m, n), F32)]
    blocks = (_nbytes((tm, tk), x.dtype) + _nbytes((tk, tn), w.dtype) + 2 * _nbytes((tm, tn), F32))
    if stats:
        out_specs += [tile, pl.BlockSpec((tm, LANES), lambda i, j, kk: (i, 0))]
        out_shape += [jax.ShapeDtypeStruct((m, n), BF16), jax.ShapeDtypeStruct((m, LANES), F32)]
        blocks += _nbytes((tm, tn), BF16)
    return pl.pallas_call(
        functools.partial(_mm_ktiled_kernel, nk=k // tk),
        grid=(m // tm, n // tn, k // tk),
        in_specs=[
            pl.BlockSpec((tm, tk), lambda i, j, kk: (i, kk)),
            pl.BlockSpec((tk, tn), lambda i, j, kk: (kk, j)),
            tile,
        ],
        out_specs=out_specs,
        out_shape=out_shape,
        compiler_params=pltpu.CompilerParams(
            dimension_semantics=("arbitrary", "arbitrary", "arbitrary"),
            vmem_limit_bytes=_vmem_limit(2 * blocks + _nbytes((tm, tn), F32))),
        name=name,
    )(x, w, residual)


def _pool_kernel(*refs, tm, seq, halo, cast_gains):
    it = iter(refs)
    x_ref, xp_ref, xn_ref, gmix_ref, gmlp_ref, w_ref, scale_ref = (next(it) for _ in range(7))
    cast_in = _take_cast_refs(it, cast_gains)
    h_ref, hn_ref = next(it), next(it)
    cast_out = [next(it) for _ in cast_gains]
    buf_ref, lvl_ref = next(it), next(it)

    i = pl.program_id(1)
    nblk = pl.num_programs(1)
    gmix = gmix_ref[...]
    x = x_ref[0]
    d_model = x.shape[-1]
    group = d_model // len(POOL_WINDOWS)

    ext = tm + 2 * halo
    buf_ref[halo:halo + tm, :] = _rmsnorm_rows(x, gmix)
    buf_ref[0:halo, :] = jnp.where(i > 0, _rmsnorm_rows(xp_ref[0], gmix), 0.0)
    buf_ref[halo + tm:ext, :] = jnp.where(i < nblk - 1, _rmsnorm_rows(xn_ref[0], gmix), 0.0)
    buf_ref[ext:ext + halo, :] = jnp.zeros((halo, d_model), F32)
    lvl_ref[:, ext:ext + halo, :] = jnp.zeros((2, halo, group), F32)

    pos = i * tm + lax.broadcasted_iota(jnp.int32, (tm, 1), 0)
    for gi, win in enumerate(POOL_WINDOWS):
        cols = slice(gi * group, (gi + 1) * group)
        half = win // 2
        load = lambda lo, n: buf_ref[lo:lo + n, cols]
        span, slot = 1, 0
        while span < half:
            lvl_ref[slot, 0:ext, :] = load(0, ext) + load(span, ext)
            load = functools.partial(lambda s, lo, n: lvl_ref[s, lo:lo + n, :], slot)
            span, slot = 2 * span, 1 - slot
        acc = load(halo - half, tm) + load(halo, tm)
        count = (jnp.minimum(pos + half, seq) - jnp.maximum(pos - half, 0)).astype(F32)
        diff = acc * (1.0 / count) - buf_ref[halo:halo + tm, cols]
        y = jnp.dot(diff.astype(BF16), w_ref[gi], preferred_element_type=F32)
        h_ref[0, :, cols] = x[:, cols] + y * scale_ref[:, cols]

    hn_ref[0] = _rmsnorm_rows(h_ref[0], gmlp_ref[...]).astype(hn_ref.dtype)
    _run_casts(cast_in, cast_out)


def pool_layer(x, g_mix, g_mlp, pool_w, pool_scale, casts=(), tm=256):
    b, s, d = x.shape
    halo = SUBLANES
    assert max(POOL_WINDOWS) // 2 <= halo and tm % halo == 0 and s % tm == 0
    ng, gd, _ = pool_w.shape
    hb = tm // halo
    grid = (b, s // tm)
    row_block = pl.BlockSpec((1, tm, d), lambda bi, i: (bi, i, 0))
    vec = pl.BlockSpec((1, d), lambda bi, i: (0, 0))
    in_specs = [
        row_block,
        pl.BlockSpec((1, halo, d), lambda bi, i: (bi, jnp.maximum(i * hb - 1, 0), 0)),
        pl.BlockSpec((1, halo, d), lambda bi, i: (bi, jnp.minimum((i + 1) * hb, s // halo - 1), 0)),
        vec,
        vec,
        pl.BlockSpec((ng, gd, gd), lambda bi, i: (0, 0, 0), pipeline_mode=pl.Buffered(1)),
        vec,
    ]
    args = [x, x, x, g_mix.reshape(1, d), g_mlp.reshape(1, d), pool_w.astype(BF16), pool_scale.reshape(1, d)]
    out_specs = [row_block, row_block]
    out_shape = [jax.ShapeDtypeStruct((b, s, d), F32), jax.ShapeDtypeStruct((b, s, d), BF16)]
    blocks = (2 * _nbytes((tm, d), F32) + _nbytes((tm, d), BF16)) * 2 + _nbytes(pool_w.shape, BF16)
    blocks += 2 * _add_casts(casts, grid, in_specs, args, out_specs, out_shape)
    scratch = [pltpu.VMEM((tm + 3 * halo, d), F32), pltpu.VMEM((2, tm + 3 * halo, gd), F32)]
    scratch_bytes = _nbytes((tm + 3 * halo, d + 2 * gd), F32)
    return pl.pallas_call(
        functools.partial(_pool_kernel, tm=tm, seq=s, halo=halo,
                          cast_gains=tuple(job.gain is not None for job in casts)),
        grid=grid,
        in_specs=in_specs,
        out_specs=out_specs,
        out_shape=out_shape,
        scratch_shapes=scratch,
        compiler_params=pltpu.CompilerParams(
            dimension_semantics=("arbitrary", "arbitrary"),
            vmem_limit_bytes=_vmem_limit(blocks + scratch_bytes)),
        name="pool_layer",
    )(*args)


def _gla_block_stages(q_ref, k_ref, v_ref, ghi_ref, glo_ref, state_ref, finish, *, head, row0, rows, dk, dv,
                       reverse):
    c = CHUNK
    ns = rows // c
    order = tuple(reversed(range(ns))) if reverse else tuple(range(ns))
    kcols = slice(head * dk, (head + 1) * dk)
    vcols = slice(head * dv, (head + 1) * dv)

    def load(ref, cols):
        return jnp.concatenate([ref[0, row0 + ci * c:row0 + (ci + 1) * c, cols] for ci in order], axis=0)

    q = load(q_ref, kcols) * (dk ** -0.5)
    k = load(k_ref, kcols)

    row = lax.broadcasted_iota(jnp.int32, (rows, rows), 0)
    col = lax.broadcasted_iota(jnp.int32, (rows, rows), 1)
    shift = c.bit_length() - 1
    row_chunk = lax.shift_right_logical(row, shift)
    col_chunk = lax.shift_right_logical(col, shift)
    causal = (col >= row) if reverse else (col <= row)
    same_chunk = (row_chunk == col_chunk) & causal
    tri = jnp.where(same_chunk, 1.0, 0.0).astype(BF16)
    b = (jnp.dot(tri, load(ghi_ref, kcols), preferred_element_type=F32)
         + jnp.dot(tri, load(glo_ref, kcols), preferred_element_type=F32))

    last_i, mid_i = (0, c - c // 2) if reverse else (c - 1, c // 2 - 1)
    b_last = [b[p * c + last_i:p * c + last_i + 1, :] for p in range(ns)]
    b_mid = [b[p * c + mid_i:p * c + mid_i + 1, :] for p in range(ns)]
    a = [jnp.exp2(x) for x in b_last]
    yield False

    def prod(factors):
        out = None
        for f in factors:
            out = f if out is None else out * f
        return out

    q_intra, k_intra, q_inter, k_state, q_prev, k_next = [], [], [], [], [], []
    q_skip = {d: [] for d in range(2, ns)}
    for p in range(ns):
        rs = slice(p * c, (p + 1) * c)
        bp, qp, kp = b[rs], q[rs], k[rs]
        qn = qp * jnp.exp2(bp)
        q_intra.append((qp * jnp.exp2(bp - b_mid[p])).astype(BF16))
        q_inter.append(qn.astype(BF16))
        before = prod(a[:p])
        q_prev.append((qn if before is None else qn * before).astype(BF16))
        for d in range(2, p + 1):
            q_skip[d].append((qn * prod(a[p - d + 1:p])).astype(BF16))
        yield False
        ks = kp * jnp.exp2(b_last[p] - bp)
        k_intra.append((kp * jnp.exp2(b_mid[p] - bp)).astype(BF16))
        k_state.append(ks.astype(BF16))
        after = prod(a[p + 1:])
        k_next.append((ks if after is None else ks * after).astype(BF16))
        yield False
    v = load(v_ref, vcols)
    yield True

    nt = (((1,), (1,)), ((), ()))
    s_intra = lax.dot_general(jnp.concatenate(q_intra, axis=0), jnp.concatenate(k_intra, axis=0), nt,
                              preferred_element_type=F32)
    yield None
    cross_lhs = q_inter + [x for d in range(2, ns) for x in q_skip[d]]
    s_cross = lax.dot_general(jnp.concatenate(cross_lhs, axis=0), jnp.concatenate(k_state, axis=0), nt,
                              preferred_element_type=F32)
    yield None

    skip_base = {}
    base = ns * c
    for d in range(2, ns):
        skip_base[d] = base
        base += (ns - d) * c
    score_rows = []
    row_in = lax.broadcasted_iota(jnp.int32, (c, rows), 0)
    col_in = lax.broadcasted_iota(jnp.int32, (c, rows), 1)
    col_chunk_in = lax.shift_right_logical(col_in, shift)
    for p in range(ns):
        rs = slice(p * c, (p + 1) * c)
        col_local = col_in - p * c
        causal_in = (col_local >= row_in) if reverse else (col_local <= row_in)
        sc = jnp.where((col_chunk_in == p) & causal_in, s_intra[rs], 0.0)
        if p >= 1:
            sc = jnp.where(col_chunk_in == p - 1, s_cross[rs], sc)
        for d in range(2, p + 1):
            off = skip_base[d] + (p - d) * c
            sc = jnp.where(col_chunk_in == p - d, s_cross[off:off + c], sc)
        score_rows.append(sc.astype(BF16))
    scores = jnp.concatenate(score_rows, axis=0)
    o = jnp.dot(scores, v, preferred_element_type=F32)
    yield None

    state = state_ref[head]
    o = o + jnp.dot(jnp.concatenate(q_prev, axis=0), state.astype(BF16), preferred_element_type=F32)
    yield None

    upd = lax.dot_general(jnp.concatenate(k_next, axis=0), v, (((0,), (0,)), ((), ())),
                          preferred_element_type=F32)
    decay_t = jnp.transpose(jnp.broadcast_to(prod(a), (LANES, dk)))
    state_ref[head] = _lane_tile(decay_t, dv) * state + upd
    yield None

    for p, ci in enumerate(order):
        finish(row0 + ci * c, o[p * c:(p + 1) * c])
        yield None


def _run_staggered(blocks):
    prev = None
    for gen in blocks:
        ready = False
        while not ready:
            ready = next(gen)
            if prev is not None:
                next(prev, None)
        if prev is not None:
            for _ in prev:
                pass
        prev = gen
    for _ in prev:
        pass


def _gla_kernel(*refs, nb, heads, sub, dk, dv, cast_gains):
    it = iter(refs)
    q_ref, k_ref, v_ref, ghi_ref, glo_ref, gate_ref = (next(it) for _ in range(6))
    cast_in = _take_cast_refs(it, cast_gains)
    o_ref = next(it)
    cast_out = [next(it) for _ in cast_gains]
    state_ref, acc_ref = next(it), next(it)

    s = pl.program_id(2)
    rows = q_ref.shape[1]
    stages = functools.partial(_gla_block_stages, q_ref, k_ref, v_ref, ghi_ref, glo_ref, state_ref,
                               rows=sub, dk=dk, dv=dv)
    starts = tuple(range(0, rows, sub))

    @pl.when((s == 0) | (s == nb))
    def _():
        state_ref[...] = jnp.zeros_like(state_ref)

    @pl.when(s < nb)
    def _():
        base = pl.multiple_of(s * rows, rows)

        def keep(head, row, piece):
            acc_ref[pl.ds(base + row, CHUNK), head * dv:(head + 1) * dv] = piece

        _run_casts(cast_in, cast_out)
        _run_staggered([stages(functools.partial(keep, head), head=head, row0=row0, reverse=False)
                        for head in range(heads) for row0 in starts])

    @pl.when(s >= nb)
    def _():
        base = pl.multiple_of((2 * nb - 1 - s) * rows, rows)

        def emit(head, row, piece):
            vcols = slice(head * dv, (head + 1) * dv)
            o = acc_ref[pl.ds(base + row, CHUNK), vcols] + piece
            o = o * lax.rsqrt(jnp.mean(o * o, axis=-1, keepdims=True) + EPS)
            gate = gate_ref[0, pl.ds(base + row, CHUNK), vcols]
            o_ref[0, row:row + CHUNK, vcols] = (o * gate).astype(o_ref.dtype)

        _run_casts(cast_in, cast_out)
        _run_staggered([stages(functools.partial(emit, head), head=head, row0=row0, reverse=True)
                        for head in range(heads) for row0 in reversed(starts)])


def gla_core(p, v, g_hi, g_lo, *, heads, casts=(), bs=1024, sub=256, heads_per_step=1):
    b, s, width = p.shape
    dv_total = width // 3
    dk_total = dv_total // 2
    dk, dv = dk_total // heads, dv_total // heads
    hp = heads_per_step
    bs = min(bs, s)
    assert heads % hp == 0 and s % bs == 0 and bs % sub == 0 and sub % CHUNK == 0
    gk, gv = hp * dk, hp * dv
    nb = s // bs
    groups = heads // hp
    kq = dk_total // gk
    kg = (2 * dk_total + dv_total) // gv
    grid = (b, groups, 2 * nb)

    def blk(i):
        return jnp.where(i < nb, i, 2 * nb - 1 - i)

    def out_blk(i):
        return jnp.where(i < nb, nb - 1, 2 * nb - 1 - i)

    decay_spec = pl.BlockSpec((1, bs, gk), lambda bi, h, i: (bi, blk(i), (i // nb) * groups + h))
    in_specs = [
        pl.BlockSpec((1, bs, gk), lambda bi, h, i: (bi, blk(i), h)),
        pl.BlockSpec((1, bs, gk), lambda bi, h, i: (bi, blk(i), kq + h)),
        pl.BlockSpec((1, bs, gv), lambda bi, h, i: (bi, blk(i), h)),
        decay_spec,
        decay_spec,
        pl.BlockSpec((1, s, gv), lambda bi, h, i: (bi, 0, kg + h)),
    ]
    args = [p, p, v, g_hi, g_lo, p]
    out_specs = [pl.BlockSpec((1, bs, gv), lambda bi, h, i: (bi, out_blk(i), h))]
    out_shape = [jax.ShapeDtypeStruct((b, s, dv_total), BF16)]
    blocks = (2 * _nbytes((bs, gk), F32) + _nbytes((s, gv), F32) + 2 * _nbytes((bs, gk), BF16)
              + 2 * _nbytes((bs, gv), BF16))
    blocks += _add_casts(casts, grid, in_specs, args, out_specs, out_shape)
    scratch_bytes = _nbytes((hp, dk, dv), F32) + _nbytes((s, gv), F32)
    return pl.pallas_call(
        functools.partial(_gla_kernel, nb=nb, heads=hp, sub=sub, dk=dk, dv=dv,
                          cast_gains=tuple(job.gain is not None for job in casts)),
        grid=grid,
        in_specs=in_specs,
        out_specs=out_specs,
        out_shape=out_shape,
        scratch_shapes=[pltpu.VMEM((hp, dk, dv), F32), pltpu.VMEM((s, gv), F32)],
        compiler_params=pltpu.CompilerParams(
            dimension_semantics=("arbitrary", "arbitrary", "arbitrary"),
            vmem_limit_bytes=_vmem_limit(2 * blocks + scratch_bytes)),
        name="gla_scan",
    )(*args)


def gla_layer(h, hb, h_ss, w_in_t, w_out_f32, layer, w_up_f, b_up_f, w_up_b, b_up_b, g_norm, next_cast):
    b, s, d = h.shape
    t = b * s
    main = w_in_t.shape[0] - 2 * GATE_RANK
    h2 = h.reshape(t, d)
    w_r_t = jnp.pad(w_in_t[main:], ((0, LANES - 2 * GATE_RANK), (0, 0)))
    gate_w = jnp.concatenate([jnp.pad(w_up_f, ((0, LANES - GATE_RANK), (0, 0))),
                              jnp.pad(w_up_b, ((GATE_RANK, LANES - 2 * GATE_RANK), (0, 0)))], axis=1).astype(BF16)
    gate_b = jnp.concatenate([b_up_f, b_up_b])
    dv_total = w_out_f32.shape[1]
    dk2 = gate_w.shape[1]
    p, _, g_hi, g_lo, v, w_out, next_bf16 = matmul(
        hb, w_in_t, F32, n=main, w_transposed=True, silu_from=main - dv_total, row_ss=h_ss, side_w=w_r_t,
        gates=(gate_w, gate_b, 1.0 / GATE_TAU), copy_bf16=(dk2, dv_total),
        casts=(CastJob(w_out_f32, layer, gain=jnp.tile(g_norm, GLA_HEADS)), next_cast), name="gla_in_proj")
    og = gla_core(p.reshape(b, s, main), v.reshape(b, s, dv_total), g_hi.reshape(b, s, dk2),
                  g_lo.reshape(b, s, dk2), heads=GLA_HEADS)[0]
    out, out_b, out_ss = matmul(og.reshape(t, -1), w_out, F32, residual=h2, stats=True, name="gla_out_proj")
    return out, out_b, out_ss, next_bf16


def kernel(x, norm_mix, norm_mlp, norm_final, pool_w, pool_scale, gla_w_in, gla_w_up_f, gla_b_up_f, gla_w_up_b,
           gla_b_up_b, gla_g_norm, gla_w_out, mlp_w_in, mlp_w_out):
    b, s, d = x.shape
    t = b * s
    w_in_t_f32 = jnp.swapaxes(gla_w_in, 1, 2)
    h, hn, w1 = pool_layer(x, norm_mix[0], norm_mlp[0], pool_w[0], pool_scale[0], casts=(CastJob(mlp_w_in, 0),))
    a, w2, w_in = matmul(hn.reshape(t, d), w1, BF16, relu2=True,
                         casts=(CastJob(mlp_w_out, 0), CastJob(w_in_t_f32, 0, gain=norm_mix[1], gain_axis=1)),
                         name="mlp_up_proj")
    h, hb, h_ss = matmul_ktiled(a, w2, h.reshape(t, d), stats=True, name="mlp_down_proj")

    h, hb, h_ss, w1 = gla_layer(h.reshape(b, s, d), hb, h_ss, w_in, gla_w_out, 0, gla_w_up_f[0], gla_b_up_f[0],
                                gla_w_up_b[0], gla_b_up_b[0], gla_g_norm[0],
                                CastJob(mlp_w_in, 1, gain=norm_mlp[1]))

    a, w2 = matmul(hb, w1, BF16, relu2=True, row_ss=h_ss, casts=(CastJob(mlp_w_out, 1),), name="mlp_up_proj")
    (h,) = matmul_ktiled(a, w2, h, name="mlp_down_proj")
    return rmsnorm(h, norm_final, F32).reshape(b, s, d)
```
